```python
import jax, jax.numpy as jnp
from jax import lax
import numpy as np

D_MODEL = 1024
BATCH = 8
SEQ = 4096
DEPTH = 1

CHUNK = 64
POOL_WINDOWS = (2, 4, 8, 16)
N_POOL_GROUPS = len(POOL_WINDOWS)
D_POOL = D_MODEL
POOL_GROUP = D_POOL // N_POOL_GROUPS
SGU_BLOCK = 128
N_SGU_HEADS = 4
D_SGU = D_MODEL
SGU_HEAD = D_SGU // N_SGU_HEADS
D_IN = D_POOL + 2 * D_SGU + D_POOL + D_SGU
D_FF = 4 * D_MODEL
EPS = 1e-6

kernel_name = "hybrid_pool_sgu_gated_block"


def rms_norm(x, g):
    xf = x.astype(jnp.float32)
    y = xf * lax.rsqrt(jnp.mean(xf * xf, axis=-1, keepdims=True) + EPS)
    return (y * g.astype(jnp.float32)).astype(x.dtype)


def layer_norm(x, g, b):
    xf = x.astype(jnp.float32)
    mu = jnp.mean(xf, axis=-1, keepdims=True)
    xc = xf - mu
    y = xc * lax.rsqrt(jnp.mean(xc * xc, axis=-1, keepdims=True) + EPS)
    return (y * g.astype(jnp.float32) + b.astype(jnp.float32)).astype(x.dtype)


def multiscale_pool(p):
    s_len = p.shape[1]
    pf = p.astype(jnp.float32)
    csum = jnp.cumsum(pf, axis=1)
    pos1 = jnp.arange(1, s_len + 1)
    outs = []
    for gi, w in enumerate(POOL_WINDOWS):
        sl = slice(gi * POOL_GROUP, (gi + 1) * POOL_GROUP)
        cg = csum[..., sl]
        prev = jnp.pad(cg, ((0, 0), (w, 0), (0, 0)))[:, :s_len]
        cnt = jnp.minimum(pos1, w).astype(jnp.float32)[None, :, None]
        outs.append((cg - prev) / cnt - pf[..., sl])
    return jnp.stack(outs, axis=2).astype(p.dtype)


def chunk_causal_block_mask():
    pos = jnp.arange(SGU_BLOCK)
    return (pos[:, None] // CHUNK) >= (pos[None, :] // CHUNK)


def setup_inputs(seed: int = 0) -> dict:
    key = jax.random.key(seed)
    ks = jax.random.split(key, 20)
    f32 = jnp.float32
    nrm = lambda k, shape, s: jax.random.normal(k, shape, f32) * s
    return {
        "x": jax.random.normal(ks[0], (BATCH, SEQ, D_MODEL), f32),
        "norm1_pre_g": 1.0 + nrm(ks[1], (D_MODEL,), 0.05),
        "w_in": nrm(ks[2], (D_MODEL, D_IN), D_MODEL ** -0.5),
        "b_in": nrm(ks[3], (D_IN,), 0.02),
        "w_pool": nrm(ks[4], (N_POOL_GROUPS, POOL_GROUP, POOL_GROUP), POOL_GROUP ** -0.5),
        "pool_scale": 1.0 + nrm(ks[5], (D_POOL,), 0.1),
        "sgu_ln_g": 1.0 + nrm(ks[6], (D_SGU,), 0.05),
        "sgu_ln_b": nrm(ks[7], (D_SGU,), 0.02),
        "w_spatial": nrm(ks[8], (N_SGU_HEADS, SGU_BLOCK, SGU_BLOCK), SGU_BLOCK ** -0.5),
        "b_spatial": 1.0 + nrm(ks[9], (N_SGU_HEADS, SGU_BLOCK), 0.02),
        "w_sgu_proj": nrm(ks[10], (N_SGU_HEADS, SGU_HEAD, SGU_HEAD), SGU_HEAD ** -0.5),
        "w_out": nrm(ks[11], (D_MODEL, D_MODEL), D_MODEL ** -0.5),
        "norm1_post_g": 1.0 + nrm(ks[12], (D_MODEL,), 0.05),
        "norm2_pre_g": 1.0 + nrm(ks[13], (D_MODEL,), 0.05),
        "w_ff1": nrm(ks[14], (D_MODEL, D_FF), D_MODEL ** -0.5),
        "w_ff2": nrm(ks[15], (D_FF, D_MODEL), D_FF ** -0.5),
        "norm2_post_g": 1.0 + nrm(ks[16], (D_MODEL,), 0.05),
    }


def reference(x, norm1_pre_g, w_in, b_in, w_pool, pool_scale, sgu_ln_g, sgu_ln_b,
              w_spatial, b_spatial, w_sgu_proj, w_out, norm1_post_g, norm2_pre_g,
              w_ff1, w_ff2, norm2_post_g):
    bsz, s_len, _ = x.shape
    n_blk = s_len // SGU_BLOCK
    h = x
    for _ in range(DEPTH):
        xn = rms_norm(h, norm1_pre_g)
        z = jnp.einsum('bsd,de->bse', xn, w_in) + b_in
        o = 0
        z_pool = z[..., o:o + D_POOL]; o += D_POOL
        z_u = z[..., o:o + D_SGU]; o += D_SGU
        z_v = z[..., o:o + D_SGU]; o += D_SGU
        z_ga = z[..., o:o + D_POOL]; o += D_POOL
        z_gb = z[..., o:o + D_SGU]

        pooled = multiscale_pool(z_pool)
        a = jnp.einsum('bsgc,gcd->bsgd', pooled, w_pool).reshape(bsz, s_len, D_POOL)
        a = a * pool_scale

        u = jax.nn.gelu(z_u)
        v = layer_norm(jax.nn.gelu(z_v), sgu_ln_g, sgu_ln_b)
        vb = v.reshape(bsz, n_blk, SGU_BLOCK, N_SGU_HEADS, SGU_HEAD)
        ws = jnp.where(chunk_causal_block_mask()[None], w_spatial, 0.0).astype(v.dtype)
        sv = jnp.einsum('hij,bnjhc->bnihc', ws, vb) + b_spatial.T[None, None, :, :, None]
        gated = u.reshape(bsz, n_blk, SGU_BLOCK, N_SGU_HEADS, SGU_HEAD) * sv
        bbr = jnp.einsum('bnihc,hcd->bnihd', gated, w_sgu_proj).reshape(bsz, s_len, D_SGU)

        merged = jax.nn.sigmoid(z_ga) * a + jax.nn.sigmoid(z_gb) * bbr
        y = jnp.einsum('bsd,de->bse', merged, w_out)
        h = h + rms_norm(y, norm1_post_g)

        hn = rms_norm(h, norm2_pre_g)
        f = jnp.square(jax.nn.relu(jnp.einsum('bsd,df->bsf', hn, w_ff1)))
        f = jnp.einsum('bsf,fd->bsd', f, w_ff2)
        h = h + rms_norm(f, norm2_post_g)
    return h
```

```python
import functools
import math

import jax
import jax.numpy as jnp
from jax import lax
from jax.experimental import pallas as pl
from jax.experimental.pallas import tpu as pltpu

EPS = 1e-6
CHUNK = 64
POOL_WINDOWS = (2, 4, 8, 16)
SGU_BLOCK = 128
HALO = 32

SEQ_TILE = 512
N_CHUNK = 512
ROW_CHUNK = 32
V7X_VMEM_LIMIT_BYTES = 56 * 1024 * 1024

_GELU_C = math.sqrt(2.0 / math.pi)
_BF16 = jnp.bfloat16
_F32 = jnp.float32


def _gelu(x):
    return 0.5 * x * (1.0 + jnp.tanh(_GELU_C * (x + 0.044715 * (x * x * x))))


def _sigmoid(x):
    return 0.5 * jnp.tanh(0.5 * x) + 0.5


def _dot(a, b):
    return jnp.dot(a, b, preferred_element_type=_F32)


def _rms_rows(x, g):
    ms = jnp.mean(x * x, axis=-1, keepdims=True)
    return x * lax.rsqrt(ms + EPS) * g


def _mix_kernel(x_ref, g_pre_ref, w_in_ref, b_in_ref, w_pool_ref, pscale_ref, ln_g_ref, ln_b_ref,
                ws_ref, bsp_ref, w_sgu_ref, w_out_ref, g_post_ref, o_ref,
                xn_scr, p_scr, s_scr, pooled_scr, gv_scr, vn_scr, u_scr, gated_scr, m_scr, bb_scr,
                merged_scr, y_scr):
    t = x_ref.shape[1]
    d = x_ref.shape[2]
    n_groups = len(POOL_WINDOWS)
    gw = d // n_groups
    seq_idx = pl.program_id(1)

    for r in range(0, t, ROW_CHUNK):
        x = x_ref[0, r:r + ROW_CHUNK, :]
        xn_scr[r:r + ROW_CHUNK, :] = _rms_rows(x, g_pre_ref[...]).astype(_BF16)

    def proj(col0):
        return (_dot(xn_scr[...], w_in_ref[:, col0:col0 + N_CHUNK])
                + b_in_ref[:, col0:col0 + N_CHUNK])

    @pl.when(seq_idx == 0)
    def _():
        p_scr[0:HALO, :] = jnp.zeros((HALO, d), _F32)

    @pl.when(seq_idx > 0)
    def _():
        p_scr[0:HALO, :] = p_scr[t:t + HALO, :]

    for c in range(0, d, N_CHUNK):
        p_scr[HALO:HALO + t, c:c + N_CHUNK] = proj(c)

    row = lax.broadcasted_iota(jnp.int32, (HALO, gw), 0)
    pos1 = seq_idx * t + row + 1
    for gi, w in enumerate(POOL_WINDOWS):
        cols = slice(gi * gw, (gi + 1) * gw)
        src, k, lo, slot = p_scr.at[:, cols], 1, 8, 0
        while 2 * k < w:
            dst = s_scr.at[slot]
            dst[lo:HALO + t, :] = src[lo:HALO + t, :] + src[lo - k:HALO + t - k, :]
            src, k, lo, slot = dst, 2 * k, lo + 8, 1 - slot
        win = src[HALO:HALO + t, :] + src[HALO - k:HALO + t - k, :]
        cur = p_scr[HALO:HALO + t, cols]
        pooled_scr[:, cols] = (win * (1.0 / w) - cur).astype(_BF16)
        cnt = jnp.minimum(pos1, w).astype(_F32)
        pooled_scr[0:HALO, cols] = (win[0:HALO] / cnt - cur[0:HALO]).astype(_BF16)

    for c in range(0, d, N_CHUNK):
        gv_scr[:, c:c + N_CHUNK] = _gelu(proj(2 * d + c))
    for r in range(0, t, ROW_CHUNK):
        gv = gv_scr[r:r + ROW_CHUNK, :]
        mu = jnp.mean(gv, axis=-1, keepdims=True)
        xc = gv - mu
        var = jnp.mean(xc * xc, axis=-1, keepdims=True)
        vn = xc * lax.rsqrt(var + EPS) * ln_g_ref[...] + ln_b_ref[...]
        vn_scr[r:r + ROW_CHUNK, :] = vn.astype(_BF16)
    for c in range(0, d, N_CHUNK):
        u_scr[:, c:c + N_CHUNK] = _gelu(proj(d + c))

    qi = lax.broadcasted_iota(jnp.int32, (SGU_BLOCK, SGU_BLOCK), 0) // CHUNK
    kj = lax.broadcasted_iota(jnp.int32, (SGU_BLOCK, SGU_BLOCK), 1) // CHUNK
    causal = qi >= kj
    for h in range(n_groups):
        cols = slice(h * gw, (h + 1) * gw)
        ws = jnp.where(causal, ws_ref[h], 0.0).astype(_BF16)
        for r in range(0, t, SGU_BLOCK):
            rows = slice(r, r + SGU_BLOCK)
            sv = _dot(ws, vn_scr[rows, cols]) + bsp_ref[h]
            gated_scr[rows, cols] = (u_scr[rows, cols] * sv).astype(_BF16)

    for gi in range(n_groups):
        cols = slice(gi * gw, (gi + 1) * gw)
        m_scr[:, cols] = _dot(pooled_scr[:, cols], w_pool_ref[gi]) * pscale_ref[:, cols]
        bb_scr[:, cols] = _dot(gated_scr[:, cols], w_sgu_ref[gi])
    for c in range(0, d, N_CHUNK):
        cs = slice(c, c + N_CHUNK)
        ga = _sigmoid(proj(3 * d + c))
        gb = _sigmoid(proj(4 * d + c))
        merged_scr[:, cs] = (ga * m_scr[:, cs] + gb * bb_scr[:, cs]).astype(_BF16)

    for c in range(0, d, N_CHUNK):
        y_scr[:, c:c + N_CHUNK] = _dot(merged_scr[...], w_out_ref[:, c:c + N_CHUNK])
    for r in range(0, t, ROW_CHUNK):
        rows = slice(r, r + ROW_CHUNK)
        o_ref[0, rows, :] = x_ref[0, rows, :] + _rms_rows(y_scr[rows, :], g_post_ref[...])


def _ffn_kernel(h_ref, g_pre_ref, w1_ref, w2_ref, g_post_ref, o_ref, hn_scr, f_scr, y_scr):
    t = h_ref.shape[0]
    d = h_ref.shape[1]
    d_ff = w1_ref.shape[1]
    for r in range(0, t, ROW_CHUNK):
        rows = slice(r, r + ROW_CHUNK)
        hn_scr[rows, :] = _rms_rows(h_ref[rows, :], g_pre_ref[...]).astype(_BF16)
    for c in range(0, d_ff, N_CHUNK):
        f = jnp.maximum(_dot(hn_scr[...], w1_ref[:, c:c + N_CHUNK]), 0.0)
        f_scr[:, c:c + N_CHUNK] = (f * f).astype(_BF16)
    for c in range(0, d, N_CHUNK):
        y_scr[:, c:c + N_CHUNK] = _dot(f_scr[...], w2_ref[:, c:c + N_CHUNK])
    for r in range(0, t, ROW_CHUNK):
        rows = slice(r, r + ROW_CHUNK)
        o_ref[rows, :] = h_ref[rows, :] + _rms_rows(y_scr[rows, :], g_post_ref[...])


def _resident(shape):
    zeros = (0,) * len(shape)
    return pl.BlockSpec(shape, lambda *_: zeros, pipeline_mode=pl.Buffered(1))


def kernel(x, norm1_pre_g, w_in, b_in, w_pool, pool_scale, sgu_ln_g, sgu_ln_b, w_spatial, b_spatial,
           w_sgu_proj, w_out, norm1_post_g, norm2_pre_g, w_ff1, w_ff2, norm2_post_g):
    bsz, s_len, d = x.shape
    d_in = w_in.shape[1]
    d_ff = w_ff1.shape[1]
    n_heads, blk, _ = w_spatial.shape
    gw = d // n_heads
    t = SEQ_TILE
    assert s_len % t == 0 and t % SGU_BLOCK == 0 and blk == SGU_BLOCK
    assert d % N_CHUNK == 0 and d_ff % N_CHUNK == 0 and d_in == 5 * d
    assert len(POOL_WINDOWS) == n_heads == w_pool.shape[0] and max(POOL_WINDOWS) <= HALO

    row = lambda v: v.reshape(1, -1).astype(_F32)
    bsp = jnp.broadcast_to(b_spatial.astype(_F32)[:, :, None], (n_heads, blk, gw))

    mix_in = (x, row(norm1_pre_g), w_in.astype(_BF16), row(b_in), w_pool.astype(_BF16), row(pool_scale),
              row(sgu_ln_g), row(sgu_ln_b), w_spatial.astype(_F32), bsp, w_sgu_proj.astype(_BF16),
              w_out.astype(_BF16), row(norm1_post_g))
    tile_spec = pl.BlockSpec((1, t, d), lambda b, s: (b, s, 0))
    h = pl.pallas_call(
        _mix_kernel,
        grid=(bsz, s_len // t),
        in_specs=[tile_spec] + [_resident(a.shape) for a in mix_in[1:]],
        out_specs=tile_spec,
        out_shape=jax.ShapeDtypeStruct(x.shape, x.dtype),
        scratch_shapes=[
            pltpu.VMEM((t, d), _BF16),
            pltpu.VMEM((t + HALO, d), _F32),
            pltpu.VMEM((2, t + HALO, gw), _F32),
            pltpu.VMEM((t, d), _BF16),
            pltpu.VMEM((t, d), _F32),
            pltpu.VMEM((t, d), _BF16),
            pltpu.VMEM((t, d), _F32),
            pltpu.VMEM((t, d), _BF16),
            pltpu.VMEM((t, d), _F32),
            pltpu.VMEM((t, d), _F32),
            pltpu.VMEM((t, d), _BF16),
            pltpu.VMEM((t, d), _F32),
        ],
        compiler_params=pltpu.CompilerParams(
            dimension_semantics=("arbitrary", "arbitrary"),
            vmem_limit_bytes=V7X_VMEM_LIMIT_BYTES),
        name="token_mix",
    )(*mix_in)

    m = bsz * s_len
    ffn_in = (h.reshape(m, d), row(norm2_pre_g), w_ff1.astype(_BF16), w_ff2.astype(_BF16),
              row(norm2_post_g))
    row_spec = pl.BlockSpec((t, d), lambda i: (i, 0))
    out = pl.pallas_call(
        _ffn_kernel,
        grid=(m // t,),
        in_specs=[row_spec] + [_resident(a.shape) for a in ffn_in[1:]],
        out_specs=row_spec,
        out_shape=jax.ShapeDtypeStruct((m, d), x.dtype),
        scratch_shapes=[
            pltpu.VMEM((t, d), _BF16),
            pltpu.VMEM((t, d_ff), _BF16),
            pltpu.VMEM((t, d), _F32),
        ],
        compiler_params=pltpu.CompilerParams(
            dimension_semantics=("arbitrary",),
            vmem_limit_bytes=V7X_VMEM_LIMIT_BYTES),
        name="channel_mix",
    )(*ffn_in)
    return out.reshape(bsz, s_len, d)
```

```python
import functools
import math

import jax
import jax.numpy as jnp
from jax import lax
from jax.experimental import pallas as pl
from jax.experimental.pallas import tpu as pltpu

EPS = 1e-6
CHUNK = 64
POOL_WINDOWS = (2, 4, 8, 16)
SGU_BLOCK = 128
HALO = 32

SEQ_TILE = 256
N_CHUNK = 512
ROW_CHUNK = 32
V7X_VMEM_LIMIT_BYTES = 56 * 1024 * 1024

_GELU_C = math.sqrt(2.0 / math.pi)
_BF16 = jnp.bfloat16
_F32 = jnp.float32


def _gelu(x):
    return 0.5 * x * (1.0 + jnp.tanh(_GELU_C * (x + 0.044715 * (x * x * x))))


def _sigmoid(x):
    return 0.5 * jnp.tanh(0.5 * x) + 0.5


def _dot(a, b):
    return jnp.dot(a, b, preferred_element_type=_F32)


def _rms_rows(x, g):
    ms = jnp.mean(x * x, axis=-1, keepdims=True)
    return x * lax.rsqrt(ms + EPS) * g


def _mix_kernel(x_ref, g_pre_ref, w_in_ref, b_in_ref, w_pool_ref, pscale_ref, ln_g_ref, ln_b_ref,
                ws_ref, bsp_ref, w_sgu_ref, w_out_ref, g_post_ref, o_ref,
                xn_scr, p_scr, s_scr, pooled_scr, gv_scr, vn_scr, u_scr, gated_scr, m_scr, bb_scr,
                merged_scr, y_scr):
    t = x_ref.shape[1]
    d = x_ref.shape[2]
    n_groups = len(POOL_WINDOWS)
    gw = d // n_groups
    seq_idx = pl.program_id(1)

    for r in range(0, t, ROW_CHUNK):
        x = x_ref[0, r:r + ROW_CHUNK, :]
        xn_scr[r:r + ROW_CHUNK, :] = _rms_rows(x, g_pre_ref[...]).astype(_BF16)

    def proj(col0):
        return (_dot(xn_scr[...], w_in_ref[:, col0:col0 + N_CHUNK])
                + b_in_ref[:, col0:col0 + N_CHUNK])

    @pl.when(seq_idx == 0)
    def _():
        p_scr[0:HALO, :] = jnp.zeros((HALO, d), _F32)

    @pl.when(seq_idx > 0)
    def _():
        p_scr[0:HALO, :] = p_scr[t:t + HALO, :]

    for c in range(0, d, N_CHUNK):
        p_scr[HALO:HALO + t, c:c + N_CHUNK] = proj(c)

    row = lax.broadcasted_iota(jnp.int32, (HALO, gw), 0)
    pos1 = seq_idx * t + row + 1
    for gi, w in enumerate(POOL_WINDOWS):
        cols = slice(gi * gw, (gi + 1) * gw)
        src, k, lo, slot = p_scr.at[:, cols], 1, 8, 0
        while 2 * k < w:
            dst = s_scr.at[slot]
            dst[lo:HALO + t, :] = src[lo:HALO + t, :] + src[lo - k:HALO + t - k, :]
            src, k, lo, slot = dst, 2 * k, lo + 8, 1 - slot
        win = src[HALO:HALO + t, :] + src[HALO - k:HALO + t - k, :]
        cur = p_scr[HALO:HALO + t, cols]
        pooled_scr[:, cols] = (win * (1.0 / w) - cur).astype(_BF16)
        cnt = jnp.minimum(pos1, w).astype(_F32)
        pooled_scr[0:HALO, cols] = (win[0:HALO] / cnt - cur[0:HALO]).astype(_BF16)

    for c in range(0, d, N_CHUNK):
        gv_scr[:, c:c + N_CHUNK] = _gelu(proj(2 * d + c))
    for r in range(0, t, ROW_CHUNK):
        gv = gv_scr[r:r + ROW_CHUNK, :]
        mu = jnp.mean(gv, axis=-1, keepdims=True)
        xc = gv - mu
        var = jnp.mean(xc * xc, axis=-1, keepdims=True)
        vn = xc * lax.rsqrt(var + EPS) * ln_g_ref[...] + ln_b_ref[...]
        vn_scr[r:r + ROW_CHUNK, :] = vn.astype(_BF16)
    for c in range(0, d, N_CHUNK):
        u_scr[:, c:c + N_CHUNK] = _gelu(proj(d + c))

    qi = lax.broadcasted_iota(jnp.int32, (SGU_BLOCK, SGU_BLOCK), 0) // CHUNK
    kj = lax.broadcasted_iota(jnp.int32, (SGU_BLOCK, SGU_BLOCK), 1) // CHUNK
    causal = qi >= kj
    for h in range(n_groups):
        cols = slice(h * gw, (h + 1) * gw)
        ws = jnp.where(causal, ws_ref[h], 0.0).astype(_BF16)
        for r in range(0, t, SGU_BLOCK):
            rows = slice(r, r + SGU_BLOCK)
            sv = _dot(ws, vn_scr[rows, cols]) + bsp_ref[h]
            gated_scr[rows, cols] = (u_scr[rows, cols] * sv).astype(_BF16)

    for gi in range(n_groups):
        cols = slice(gi * gw, (gi + 1) * gw)
        m_scr[:, cols] = _dot(pooled_scr[:, cols], w_pool_ref[gi]) * pscale_ref[:, cols]
        bb_scr[:, cols] = _dot(gated_scr[:, cols], w_sgu_ref[gi])
    for c in range(0, d, N_CHUNK):
        cs = slice(c, c + N_CHUNK)
        ga = _sigmoid(proj(3 * d + c))
        gb = _sigmoid(proj(4 * d + c))
        merged_scr[:, cs] = (ga * m_scr[:, cs] + gb * bb_scr[:, cs]).astype(_BF16)

    for c in range(0, d, N_CHUNK):
        y_scr[:, c:c + N_CHUNK] = _dot(merged_scr[...], w_out_ref[:, c:c + N_CHUNK])
    for r in range(0, t, ROW_CHUNK):
        rows = slice(r, r + ROW_CHUNK)
        o_ref[0, rows, :] = x_ref[0, rows, :] + _rms_rows(y_scr[rows, :], g_post_ref[...])


def _ffn_kernel(h_ref, g_pre_ref, w1_ref, w2_ref, g_post_ref, o_ref, hn_scr, f_scr, y_scr):
    t = h_ref.shape[0]
    d = h_ref.shape[1]
    d_ff = w1_ref.shape[1]
    for r in range(0, t, ROW_CHUNK):
        rows = slice(r, r + ROW_CHUNK)
        hn_scr[rows, :] = _rms_rows(h_ref[rows, :], g_pre_ref[...]).astype(_BF16)
    for c in range(0, d_ff, N_CHUNK):
        f = jnp.maximum(_dot(hn_scr[...], w1_ref[:, c:c + N_CHUNK]), 0.0)
        f_scr[:, c:c + N_CHUNK] = (f * f).astype(_BF16)
    for c in range(0, d, N_CHUNK):
        y_scr[:, c:c + N_CHUNK] = _dot(f_scr[...], w2_ref[:, c:c + N_CHUNK])
    for r in range(0, t, ROW_CHUNK):
        rows = slice(r, r + ROW_CHUNK)
        o_ref[rows, :] = h_ref[rows, :] + _rms_rows(y_scr[rows, :], g_post_ref[...])


def _resident(shape):
    zeros = (0,) * len(shape)
    return pl.BlockSpec(shape, lambda *_: zeros, pipeline_mode=pl.Buffered(1))


def kernel(x, norm1_pre_g, w_in, b_in, w_pool, pool_scale, sgu_ln_g, sgu_ln_b, w_spatial, b_spatial,
           w_sgu_proj, w_out, norm1_post_g, norm2_pre_g, w_ff1, w_ff2, norm2_post_g):
    bsz, s_len, d = x.shape
    d_in = w_in.shape[1]
    d_ff = w_ff1.shape[1]
    n_heads, blk, _ = w_spatial.shape
    gw = d // n_heads
    t = SEQ_TILE
    assert s_len % t == 0 and t % SGU_BLOCK == 0 and blk == SGU_BLOCK
    assert d % N_CHUNK == 0 and d_ff % N_CHUNK == 0 and d_in == 5 * d
    assert len(POOL_WINDOWS) == n_heads == w_pool.shape[0] and max(POOL_WINDOWS) <= HALO

    row = lambda v: v.reshape(1, -1).astype(_F32)
    bsp = jnp.broadcast_to(b_spatial.astype(_F32)[:, :, None], (n_heads, blk, gw))

    mix_in = (x, row(norm1_pre_g), w_in.astype(_BF16), row(b_in), w_pool.astype(_BF16), row(pool_scale),
              row(sgu_ln_g), row(sgu_ln_b), w_spatial.astype(_F32), bsp, w_sgu_proj.astype(_BF16),
              w_out.astype(_BF16), row(norm1_post_g))
    tile_spec = pl.BlockSpec((1, t, d), lambda b, s: (b, s, 0))
    h = pl.pallas_call(
        _mix_kernel,
        grid=(bsz, s_len // t),
        in_specs=[tile_spec] + [_resident(a.shape) for a in mix_in[1:]],
        out_specs=tile_spec,
        out_shape=jax.ShapeDtypeStruct(x.shape, x.dtype),
        scratch_shapes=[
            pltpu.VMEM((t, d), _BF16),
            pltpu.VMEM((t + HALO, d), _F32),
            pltpu.VMEM((2, t + HALO, gw), _F32),
            pltpu.VMEM((t, d), _BF16),
            pltpu.VMEM((t, d), _F32),
            pltpu.VMEM((t, d), _BF16),
            pltpu.VMEM((t, d), _F32),
            pltpu.VMEM((t, d), _BF16),
            pltpu.VMEM((t, d), _F32),
            pltpu.VMEM((t, d), _F32),
            pltpu.VMEM((t, d), _BF16),
            pltpu.VMEM((t, d), _F32),
        ],
        compiler_params=pltpu.CompilerParams(
            dimension_semantics=("arbitrary", "arbitrary"),
            vmem_limit_bytes=V7X_VMEM_LIMIT_BYTES),
        name="token_mix",
    )(*mix_in)

    m = bsz * s_len
    ffn_in = (h.reshape(m, d), row(norm2_pre_g), w_ff1.astype(_BF16), w_ff2.astype(_BF16),
              row(norm2_post_g))
    row_spec = pl.BlockSpec((t, d), lambda i: (i, 0))
    out = pl.pallas_call(
        _ffn_kernel,
        grid=(m // t,),
        in_specs=[row_spec] + [_resident(a.shape) for a in ffn_in[1:]],
        out_specs=row_spec,
        out_shape=jax.ShapeDtypeStruct((m, d), x.dtype),
        scratch_shapes=[
            pltpu.VMEM((t, d), _BF16),
            pltpu.VMEM((t, d_ff), _BF16),
            pltpu.VMEM((t, d), _F32),
        ],
        compiler_params=pltpu.CompilerParams(
            dimension_semantics=("arbitrary",),
            vmem_limit_bytes=V7X_VMEM_LIMIT_BYTES),
        name="channel_mix",
    )(*ffn_in)
    return out.reshape(bsz, s_len, d)
```

```python
import math

import jax
import jax.numpy as jnp
from jax import lax
from jax.experimental import pallas as pl
from jax.experimental.pallas import tpu as pltpu

EPS = 1e-6
CHUNK = 64
POOL_WINDOWS = (2, 4, 8, 16)
SGU_BLOCK = 128
HALO = 32

SEQ_TILE = 512
N_CHUNK = 512
ROW_CHUNK = 32
V7X_VMEM_LIMIT_BYTES = 56 * 1024 * 1024

_GELU_C = math.sqrt(2.0 / math.pi)
_BF16 = jnp.bfloat16
_F32 = jnp.float32


def _gelu(x):
    inner = x * (_GELU_C + (0.044715 * _GELU_C) * (x * x))
    hx = 0.5 * x
    return hx + hx * jnp.tanh(inner)


def _dot(a, b):
    return jnp.dot(a, b, preferred_element_type=_F32)


def _rms_rows(x, g):
    ms = jnp.mean(x * x, axis=-1, keepdims=True)
    return x * lax.rsqrt(ms + EPS) * g


def _fold_pool_kernel(w_p_ref, b_p_ref, w_pool_ref, pscale_ref, wf_ref, bf_ref):
    n_groups, gw, _ = w_pool_ref.shape
    for gi in range(n_groups):
        cols = slice(gi * gw, (gi + 1) * gw)
        half_scale = 0.5 * pscale_ref[:, cols]
        wg = w_pool_ref[gi]
        wf = jnp.dot(w_p_ref[:, cols], wg, preferred_element_type=_F32, precision=lax.Precision.HIGHEST)
        wf_ref[:, cols] = (wf * half_scale).astype(_BF16)
        b8 = jnp.broadcast_to(b_p_ref[:, cols], (8, gw))
        bf = jnp.dot(b8, wg, preferred_element_type=_F32, precision=lax.Precision.HIGHEST)
        bf_ref[:, cols] = bf[0:1] * half_scale


def _mix_kernel(x_ref, g_pre_ref, w_fold_ref, b_fold_ref, w_in_ref, b_in_ref, ln_g_ref, ln_b_ref,
                ws_ref, bsp_ref, w_sgu_ref, w_out_ref, g_post_ref, o_ref,
                xn_scr, p_scr, s_scr, gv_scr, vn_scr, u_scr, gated_scr, m_scr, bb_scr, merged_scr, y_scr):
    t = x_ref.shape[1]
    d = x_ref.shape[2]
    n_groups = len(POOL_WINDOWS)
    gw = d // n_groups
    seq_idx = pl.program_id(1)

    for r in range(0, t, ROW_CHUNK):
        x = x_ref[0, r:r + ROW_CHUNK, :]
        xn_scr[r:r + ROW_CHUNK, :] = _rms_rows(x, g_pre_ref[...]).astype(_BF16)

    def proj(col0):
        return (_dot(xn_scr[...], w_in_ref[:, col0:col0 + N_CHUNK])
                + b_in_ref[:, col0:col0 + N_CHUNK])

    @pl.when(seq_idx == 0)
    def _():
        p_scr[0:HALO, :] = jnp.zeros((HALO, d), _F32)

    @pl.when(seq_idx > 0)
    def _():
        p_scr[0:HALO, :] = p_scr[t:t + HALO, :]

    for c in range(0, d, N_CHUNK):
        cs = slice(c, c + N_CHUNK)
        p_scr[HALO:HALO + t, cs] = _dot(xn_scr[...], w_fold_ref[:, cs]) + b_fold_ref[:, cs]

    row = lax.broadcasted_iota(jnp.int32, (HALO, gw), 0)
    pos1 = seq_idx * t + row + 1
    for gi, w in enumerate(POOL_WINDOWS):
        cols = slice(gi * gw, (gi + 1) * gw)
        src, k, lo, slot = p_scr.at[:, cols], 1, 8, 0
        while 2 * k < w:
            dst = s_scr.at[slot]
            dst[lo:HALO + t, :] = src[lo:HALO + t, :] + src[lo - k:HALO + t - k, :]
            src, k, lo, slot = dst, 2 * k, lo + 8, 1 - slot
        win = src[HALO:HALO + t, :] + src[HALO - k:HALO + t - k, :]
        cur = p_scr[HALO:HALO + t, cols]
        m_scr[:, cols] = win * (1.0 / w) - cur
        cnt = jnp.minimum(pos1, w).astype(_F32)
        m_scr[0:HALO, cols] = win[0:HALO] / cnt - cur[0:HALO]

    for c in range(0, d, N_CHUNK):
        gv_scr[:, c:c + N_CHUNK] = _gelu(proj(d + c))
    for r in range(0, t, ROW_CHUNK):
        gv = gv_scr[r:r + ROW_CHUNK, :]
        mu = jnp.mean(gv, axis=-1, keepdims=True)
        xc = gv - mu
        var = jnp.mean(xc * xc, axis=-1, keepdims=True)
        vn = xc * lax.rsqrt(var + EPS) * ln_g_ref[...] + ln_b_ref[...]
        vn_scr[r:r + ROW_CHUNK, :] = vn.astype(_BF16)
    for c in range(0, d, N_CHUNK):
        u_scr[:, c:c + N_CHUNK] = _gelu(proj(c))

    qi = lax.broadcasted_iota(jnp.int32, (SGU_BLOCK, SGU_BLOCK), 0) // CHUNK
    kj = lax.broadcasted_iota(jnp.int32, (SGU_BLOCK, SGU_BLOCK), 1) // CHUNK
    causal = qi >= kj
    for h in range(n_groups):
        cols = slice(h * gw, (h + 1) * gw)
        half_ws = jnp.where(causal, 0.5 * ws_ref[h], 0.0).astype(_BF16)
        half_b = 0.5 * bsp_ref[h]
        for r in range(0, t, SGU_BLOCK):
            rows = slice(r, r + SGU_BLOCK)
            half_sv = _dot(half_ws, vn_scr[rows, cols]) + half_b
            gated_scr[rows, cols] = (u_scr[rows, cols] * half_sv).astype(_BF16)
    for h in range(n_groups):
        cols = slice(h * gw, (h + 1) * gw)
        bb_scr[:, cols] = _dot(gated_scr[:, cols], w_sgu_ref[h])

    for c in range(0, d, N_CHUNK):
        cs = slice(c, c + N_CHUNK)
        ta = jnp.tanh(0.5 * proj(2 * d + c))
        tb = jnp.tanh(0.5 * proj(3 * d + c))
        ha = m_scr[:, cs]
        hb = bb_scr[:, cs]
        merged_scr[:, cs] = ((ha + hb) + (ha * ta + hb * tb)).astype(_BF16)

    for c in range(0, d, N_CHUNK):
        y_scr[:, c:c + N_CHUNK] = _dot(merged_scr[...], w_out_ref[:, c:c + N_CHUNK])
    for r in range(0, t, ROW_CHUNK):
        rows = slice(r, r + ROW_CHUNK)
        o_ref[0, rows, :] = x_ref[0, rows, :] + _rms_rows(y_scr[rows, :], g_post_ref[...])


def _ffn_kernel(h_ref, g_pre_ref, w1_ref, w2_ref, g_post_ref, o_ref, hn_scr, f_scr, y_scr):
    t = h_ref.shape[0]
    d = h_ref.shape[1]
    d_ff = w1_ref.shape[1]
    for r in range(0, t, ROW_CHUNK):
        rows = slice(r, r + ROW_CHUNK)
        hn_scr[rows, :] = _rms_rows(h_ref[rows, :], g_pre_ref[...]).astype(_BF16)
    for c in range(0, d_ff, N_CHUNK):
        f = jnp.maximum(_dot(hn_scr[...], w1_ref[:, c:c + N_CHUNK]), 0.0)
        f_scr[:, c:c + N_CHUNK] = (f * f).astype(_BF16)
    for c in range(0, d, N_CHUNK):
        y_scr[:, c:c + N_CHUNK] = _dot(f_scr[...], w2_ref[:, c:c + N_CHUNK])
    for r in range(0, t, ROW_CHUNK):
        rows = slice(r, r + ROW_CHUNK)
        o_ref[rows, :] = h_ref[rows, :] + _rms_rows(y_scr[rows, :], g_post_ref[...])


def _resident(shape):
    zeros = (0,) * len(shape)
    return pl.BlockSpec(shape, lambda *_: zeros, pipeline_mode=pl.Buffered(1))


def kernel(x, norm1_pre_g, w_in, b_in, w_pool, pool_scale, sgu_ln_g, sgu_ln_b, w_spatial, b_spatial,
           w_sgu_proj, w_out, norm1_post_g, norm2_pre_g, w_ff1, w_ff2, norm2_post_g):
    bsz, s_len, d = x.shape
    d_in = w_in.shape[1]
    d_ff = w_ff1.shape[1]
    n_heads, blk, _ = w_spatial.shape
    gw = d // n_heads
    t = SEQ_TILE
    assert s_len % t == 0 and t % SGU_BLOCK == 0 and blk == SGU_BLOCK
    assert d % N_CHUNK == 0 and d_ff % N_CHUNK == 0 and d_in == 5 * d
    assert len(POOL_WINDOWS) == n_heads == w_pool.shape[0] and max(POOL_WINDOWS) <= HALO

    row = lambda v: v.reshape(1, -1).astype(_F32)
    b_in_row = row(b_in)

    w_fold, b_fold = pl.pallas_call(
        _fold_pool_kernel,
        grid=(1,),
        in_specs=[pl.BlockSpec((d, d), lambda i: (0, 0)), pl.BlockSpec((1, d), lambda i: (0, 0)),
                  pl.BlockSpec(w_pool.shape, lambda i: (0, 0, 0)), pl.BlockSpec((1, d), lambda i: (0, 0))],
        out_specs=[pl.BlockSpec((d, d), lambda i: (0, 0)), pl.BlockSpec((1, d), lambda i: (0, 0))],
        out_shape=[jax.ShapeDtypeStruct((d, d), _BF16), jax.ShapeDtypeStruct((1, d), _F32)],
        name="fold_pool",
    )(w_in.astype(_F32), b_in_row, w_pool.astype(_F32), row(pool_scale))

    bsp = jnp.broadcast_to(b_spatial.astype(_F32)[:, :, None], (n_heads, blk, gw))

    mix_in = (x, row(norm1_pre_g), w_fold, b_fold, w_in[:, d:].astype(_BF16), b_in_row[:, d:],
              row(sgu_ln_g), row(sgu_ln_b), w_spatial.astype(_F32), bsp, w_sgu_proj.astype(_BF16),
              w_out.astype(_BF16), row(norm1_post_g))
    tile_spec = pl.BlockSpec((1, t, d), lambda b, s: (b, s, 0))
    h = pl.pallas_call(
        _mix_kernel,
        grid=(bsz, s_len // t),
        in_specs=[tile_spec] + [_resident(a.shape) for a in mix_in[1:]],
        out_specs=tile_spec,
        out_shape=jax.ShapeDtypeStruct(x.shape, x.dtype),
        scratch_shapes=[
            pltpu.VMEM((t, d), _BF16),
            pltpu.VMEM((t + HALO, d), _F32),
            pltpu.VMEM((2, t + HALO, gw), _F32),
            pltpu.VMEM((t, d), _F32),
            pltpu.VMEM((t, d), _BF16),
            pltpu.VMEM((t, d), _F32),
            pltpu.VMEM((t, d), _BF16),
            pltpu.VMEM((t, d), _F32),
            pltpu.VMEM((t, d), _F32),
            pltpu.VMEM((t, d), _BF16),
            pltpu.VMEM((t, d), _F32),
        ],
        compiler_params=pltpu.CompilerParams(
            dimension_semantics=("arbitrary", "arbitrary"),
            vmem_limit_bytes=V7X_VMEM_LIMIT_BYTES),
        name="token_mix",
    )(*mix_in)

    m = bsz * s_len
    ffn_in = (h.reshape(m, d), row(norm2_pre_g), w_ff1.astype(_BF16), w_ff2.astype(_BF16),
              row(norm2_post_g))
    row_spec = pl.BlockSpec((t, d), lambda i: (i, 0))
    out = pl.pallas_call(
        _ffn_kernel,
        grid=(m // t,),
        in_specs=[row_spec] + [_resident(a.shape) for a in ffn_in[1:]],
        out_specs=row_spec,
        out_shape=jax.ShapeDtypeStruct((m, d), x.dtype),
        scratch_shapes=[
            pltpu.VMEM((t, d), _BF16),
            pltpu.VMEM((t, d_ff), _BF16),
            pltpu.VMEM((t, d), _F32),
        ],
        compiler_params=pltpu.CompilerParams(
            dimension_semantics=("arbitrary",),
            vmem_limit_bytes=V7X_VMEM_LIMIT_BYTES),
        name="channel_mix",
    )(*ffn_in)
    return out.reshape(bsz, s_len, d)
```

```python
import math

import jax
import jax.numpy as jnp
from jax import lax
from jax.experimental import pallas as pl
from jax.experimental.pallas import tpu as pltpu

EPS = 1e-6
CHUNK = 64
POOL_WINDOWS = (2, 4, 8, 16)
SGU_BLOCK = 128
HALO = 32

SEQ_TILE = 512
N_CHUNK = 512
ROW_CHUNK = 32
OUT_ROW_SPLIT = 2
SUB_TILES = 2
V7X_VMEM_LIMIT_BYTES = 56 * 1024 * 1024

_GELU_C = math.sqrt(2.0 / math.pi)
_BF16 = jnp.bfloat16
_F32 = jnp.float32


def _gelu(x):
    inner = x * (_GELU_C + (0.044715 * _GELU_C) * (x * x))
    hx = 0.5 * x
    return hx + hx * jnp.tanh(inner)


def _dot(a, b):
    return jnp.dot(a, b, preferred_element_type=_F32)


def _rms_rows(x, g):
    ms = jnp.mean(x * x, axis=-1, keepdims=True)
    return x * lax.rsqrt(ms + EPS) * g


def _project_norm_residual(lhs_ref, w_ref, y_scr, res_ref, g_ref, o_ref):
    t, d = y_scr.shape
    rows_per_group = t // OUT_ROW_SPLIT
    for r0 in range(0, t, rows_per_group):
        group = slice(r0, r0 + rows_per_group)
        for c in range(0, d, N_CHUNK):
            y_scr[group, c:c + N_CHUNK] = _dot(lhs_ref[group, :], w_ref[:, c:c + N_CHUNK])
        for r in range(r0, r0 + rows_per_group, ROW_CHUNK):
            rows = slice(r, r + ROW_CHUNK)
            o_ref[rows, :] = res_ref[rows, :] + _rms_rows(y_scr[rows, :], g_ref[...])


def _fold_pool_kernel(w_p_ref, b_p_ref, w_pool_ref, pscale_ref, wf_ref, bf_ref):
    gw = w_pool_ref.shape[1]
    half_scale = 0.5 * pscale_ref[...]
    wg = w_pool_ref[0]
    wf = jnp.dot(w_p_ref[...], wg, preferred_element_type=_F32, precision=lax.Precision.HIGHEST)
    wf_ref[...] = (wf * half_scale).astype(_BF16)
    b8 = jnp.broadcast_to(b_p_ref[...], (8, gw))
    bf = jnp.dot(b8, wg, preferred_element_type=_F32, precision=lax.Precision.HIGHEST)
    bf_ref[...] = bf[0:1] * half_scale


def _mix_rows(r0, tt, seq_idx, x_tile, o_tile, g_pre_ref, w_fold_ref, b_fold_ref, w_in_ref, b_in_ref,
              ln_g_ref, ln_b_ref, ws_ref, bsp_ref, w_sgu_ref, w_out_ref, g_post_ref,
              xn_scr, p_scr, s_scr, gv_scr, vn_scr, u_scr, gated_scr, m_scr, bb_scr, merged_scr, y_scr):
    d = x_tile.shape[1]
    n_groups = len(POOL_WINDOWS)
    gw = d // n_groups
    rs = slice(r0, r0 + tt)

    for r in range(r0, r0 + tt, ROW_CHUNK):
        rows = slice(r, r + ROW_CHUNK)
        xn_scr[rows, :] = _rms_rows(x_tile[rows, :], g_pre_ref[...]).astype(_BF16)

    def proj(col0):
        return (_dot(xn_scr[rs, :], w_in_ref[:, col0:col0 + N_CHUNK])
                + b_in_ref[:, col0:col0 + N_CHUNK])

    for c in range(0, d, N_CHUNK):
        cs = slice(c, c + N_CHUNK)
        p_scr[r0 + HALO:r0 + HALO + tt, cs] = _dot(xn_scr[rs, :], w_fold_ref[:, cs]) + b_fold_ref[:, cs]

    hi = r0 + HALO + tt
    for gi, w in enumerate(POOL_WINDOWS):
        cols = slice(gi * gw, (gi + 1) * gw)
        src, k, lo, slot = p_scr.at[:, cols], 1, r0 + 8, 0
        while 2 * k < w:
            dst = s_scr.at[slot]
            dst[lo:hi, :] = src[lo:hi, :] + src[lo - k:hi - k, :]
            src, k, lo, slot = dst, 2 * k, lo + 8, 1 - slot
        win = src[r0 + HALO:hi, :] + src[r0 + HALO - k:hi - k, :]
        cur = p_scr[r0 + HALO:hi, cols]
        m_scr[rs, cols] = win * (1.0 / w) - cur
        if r0 == 0:
            row = lax.broadcasted_iota(jnp.int32, (HALO, gw), 0)
            cnt = jnp.minimum(seq_idx * SEQ_TILE + row + 1, w).astype(_F32)
            m_scr[0:HALO, cols] = win[0:HALO] / cnt - cur[0:HALO]

    for c in range(0, d, N_CHUNK):
        gv_scr[rs, c:c + N_CHUNK] = _gelu(proj(2 * d + c))
    for r in range(r0, r0 + tt, ROW_CHUNK):
        gv = gv_scr[r:r + ROW_CHUNK, :]
        mu = jnp.mean(gv, axis=-1, keepdims=True)
        xc = gv - mu
        var = jnp.mean(xc * xc, axis=-1, keepdims=True)
        vn = xc * lax.rsqrt(var + EPS) * ln_g_ref[...] + ln_b_ref[...]
        vn_scr[r:r + ROW_CHUNK, :] = vn.astype(_BF16)
    for c in range(0, d, N_CHUNK):
        u_scr[rs, c:c + N_CHUNK] = _gelu(proj(d + c))

    qi = lax.broadcasted_iota(jnp.int32, (SGU_BLOCK, SGU_BLOCK), 0) // CHUNK
    kj = lax.broadcasted_iota(jnp.int32, (SGU_BLOCK, SGU_BLOCK), 1) // CHUNK
    causal = qi >= kj
    for h in range(n_groups):
        cols = slice(h * gw, (h + 1) * gw)
        half_ws = jnp.where(causal, 0.5 * ws_ref[h], 0.0).astype(_BF16)
        half_b = 0.5 * bsp_ref[h]
        for r in range(r0, r0 + tt, SGU_BLOCK):
            rows = slice(r, r + SGU_BLOCK)
            half_sv = _dot(half_ws, vn_scr[rows, cols]) + half_b
            gated_scr[rows, cols] = (u_scr[rows, cols] * half_sv).astype(_BF16)
    for h in range(n_groups):
        cols = slice(h * gw, (h + 1) * gw)
        bb_scr[rs, cols] = _dot(gated_scr[rs, cols], w_sgu_ref[h])

    for c in range(0, d, N_CHUNK):
        cs = slice(c, c + N_CHUNK)
        ta = jnp.tanh(0.5 * proj(3 * d + c))
        tb = jnp.tanh(0.5 * proj(4 * d + c))
        ha = m_scr[rs, cs]
        hb = bb_scr[rs, cs]
        merged_scr[rs, cs] = ((ha + hb) + (ha * ta + hb * tb)).astype(_BF16)

    for c in range(0, d, N_CHUNK):
        y_scr[rs, c:c + N_CHUNK] = _dot(merged_scr[rs, :], w_out_ref[:, c:c + N_CHUNK])
    for r in range(r0, r0 + tt, ROW_CHUNK):
        rows = slice(r, r + ROW_CHUNK)
        o_tile[rows, :] = x_tile[rows, :] + _rms_rows(y_scr[rows, :], g_post_ref[...])


def _mix_kernel(x_ref, *refs):
    o_ref, p_scr = refs[12], refs[14]
    weights, scratch = refs[:12], refs[13:]
    t, d = x_ref.shape[1], x_ref.shape[2]
    seq_idx = pl.program_id(1)

    @pl.when(seq_idx == 0)
    def _():
        p_scr[0:HALO, :] = jnp.zeros((HALO, d), _F32)

    @pl.when(seq_idx > 0)
    def _():
        p_scr[0:HALO, :] = p_scr[t:t + HALO, :]

    tt = t // SUB_TILES
    for r0 in range(0, t, tt):
        _mix_rows(r0, tt, seq_idx, x_ref.at[0], o_ref.at[0], *weights, *scratch)


def _ffn_kernel(h_ref, g_pre_ref, w1_ref, w2_ref, g_post_ref, o_ref, hn_scr, f_scr, y_scr):
    t = h_ref.shape[0]
    d_ff = w1_ref.shape[1]
    for r in range(0, t, ROW_CHUNK):
        rows = slice(r, r + ROW_CHUNK)
        hn_scr[rows, :] = _rms_rows(h_ref[rows, :], g_pre_ref[...]).astype(_BF16)
    for c in range(0, d_ff, N_CHUNK):
        f = jnp.maximum(_dot(hn_scr[...], w1_ref[:, c:c + N_CHUNK]), 0.0)
        f_scr[:, c:c + N_CHUNK] = (f * f).astype(_BF16)
    _project_norm_residual(f_scr, w2_ref, y_scr, h_ref, g_post_ref, o_ref)


def _resident(shape):
    zeros = (0,) * len(shape)
    return pl.BlockSpec(shape, lambda *_: zeros, pipeline_mode=pl.Buffered(1))


def kernel(x, norm1_pre_g, w_in, b_in, w_pool, pool_scale, sgu_ln_g, sgu_ln_b, w_spatial, b_spatial,
           w_sgu_proj, w_out, norm1_post_g, norm2_pre_g, w_ff1, w_ff2, norm2_post_g):
    bsz, s_len, d = x.shape
    d_in = w_in.shape[1]
    d_ff = w_ff1.shape[1]
    n_heads, blk, _ = w_spatial.shape
    gw = d // n_heads
    t = SEQ_TILE
    assert s_len % t == 0 and t % SGU_BLOCK == 0 and blk == SGU_BLOCK
    assert t % (OUT_ROW_SPLIT * ROW_CHUNK) == 0
    assert d % N_CHUNK == 0 and d_ff % N_CHUNK == 0 and d_in == 5 * d
    assert len(POOL_WINDOWS) == n_heads == w_pool.shape[0] and max(POOL_WINDOWS) <= HALO

    row = lambda v: v.reshape(1, -1).astype(_F32)
    b_in_row = row(b_in)

    w_fold, b_fold = pl.pallas_call(
        _fold_pool_kernel,
        grid=(n_heads,),
        in_specs=[pl.BlockSpec((d, gw), lambda g: (0, g)), pl.BlockSpec((1, gw), lambda g: (0, g)),
                  pl.BlockSpec((1, gw, gw), lambda g: (g, 0, 0)), pl.BlockSpec((1, gw), lambda g: (0, g))],
        out_specs=[pl.BlockSpec((d, gw), lambda g: (0, g)), pl.BlockSpec((1, gw), lambda g: (0, g))],
        out_shape=[jax.ShapeDtypeStruct((d, d), _BF16), jax.ShapeDtypeStruct((1, d), _F32)],
        name="fold_pool",
    )(w_in.astype(_F32), b_in_row, w_pool.astype(_F32), row(pool_scale))

    bsp = jnp.broadcast_to(b_spatial.astype(_F32)[:, :, None], (n_heads, blk, gw))

    mix_in = (x, row(norm1_pre_g), w_fold, b_fold, w_in.astype(_BF16), b_in_row,
              row(sgu_ln_g), row(sgu_ln_b), w_spatial.astype(_F32), bsp, w_sgu_proj.astype(_BF16),
              w_out.astype(_BF16), row(norm1_post_g))
    tile_spec = pl.BlockSpec((1, t, d), lambda b, s: (b, s, 0))
    h = pl.pallas_call(
        _mix_kernel,
        grid=(bsz, s_len // t),
        in_specs=[tile_spec] + [_resident(a.shape) for a in mix_in[1:]],
        out_specs=tile_spec,
        out_shape=jax.ShapeDtypeStruct(x.shape, x.dtype),
        scratch_shapes=[
            pltpu.VMEM((t, d), _BF16),
            pltpu.VMEM((t + HALO, d), _F32),
            pltpu.VMEM((2, t + HALO, gw), _F32),
            pltpu.VMEM((t, d), _F32),
            pltpu.VMEM((t, d), _BF16),
            pltpu.VMEM((t, d), _F32),
            pltpu.VMEM((t, d), _BF16),
            pltpu.VMEM((t, d), _F32),
            pltpu.VMEM((t, d), _F32),
            pltpu.VMEM((t, d), _BF16),
            pltpu.VMEM((t, d), _F32),
        ],
        compiler_params=pltpu.CompilerParams(
            dimension_semantics=("arbitrary", "arbitrary"),
            vmem_limit_bytes=V7X_VMEM_LIMIT_BYTES),
        name="token_mix",
    )(*mix_in)

    m = bsz * s_len
    ffn_in = (h.reshape(m, d), row(norm2_pre_g), w_ff1.astype(_BF16), w_ff2.astype(_BF16),
              row(norm2_post_g))
    row_spec = pl.BlockSpec((t, d), lambda i: (i, 0))
    out = pl.pallas_call(
        _ffn_kernel,
        grid=(m // t,),
        in_specs=[row_spec] + [_resident(a.shape) for a in ffn_in[1:]],
        out_specs=row_spec,
        out_shape=jax.ShapeDtypeStruct((m, d), x.dtype),
        scratch_shapes=[
            pltpu.VMEM((t, d), _BF16),
            pltpu.VMEM((t, d_ff), _BF16),
            pltpu.VMEM((t, d), _F32),
        ],
        compiler_params=pltpu.CompilerParams(
            dimension_semantics=("arbitrary",),
            vmem_limit_bytes=V7X_VMEM_LIMIT_BYTES),
        name="channel_mix",
    )(*ffn_in)
    return out.reshape(bsz, s_len, d)
```

```python
import math

import jax
import jax.numpy as jnp
from jax import lax
from jax.experimental import pallas as pl
from jax.experimental.pallas import tpu as pltpu

EPS = 1e-6
CHUNK = 64
POOL_WINDOWS = (2, 4, 8, 16)
SGU_BLOCK = 128
HALO = 32

SEQ_TILE = 512
N_CHUNK = 512
ROW_CHUNK = 32
W_STAGE_ROWS = 128
V7X_VMEM_LIMIT_BYTES = 56 * 1024 * 1024

_GELU_C = math.sqrt(2.0 / math.pi)
_BF16 = jnp.bfloat16
_F32 = jnp.float32


def _gelu(x):
    inner = x * (_GELU_C + (0.044715 * _GELU_C) * (x * x))
    hx = 0.5 * x
    return hx + hx * jnp.tanh(inner)


def _dot(a, b):
    return jnp.dot(a, b, preferred_element_type=_F32)


def _rms_rows(x, g):
    ms = jnp.mean(x * x, axis=-1, keepdims=True)
    return x * lax.rsqrt(ms + EPS) * g


def _row_chunks(src_hbm, dst_vmem):
    return [(src_hbm.at[pl.ds(r, W_STAGE_ROWS), :], dst_vmem.at[pl.ds(r, W_STAGE_ROWS), :])
            for r in range(0, dst_vmem.shape[0], W_STAGE_ROWS)]


def _cast_weights_once(jobs, stage, sem):
    def fetch(k):
        src, dst = jobs[k]
        return pltpu.make_async_copy(src, stage.at[k % 2, :, pl.ds(0, dst.shape[1])], sem.at[k % 2])

    fetch(0).start()
    for k, (_, dst) in enumerate(jobs):
        if k + 1 < len(jobs):
            fetch(k + 1).start()
        fetch(k).wait()
        dst[...] = stage[k % 2, :, 0:dst.shape[1]].astype(_BF16)


def _fold_pool_kernel(w_p_ref, b_p_ref, w_pool_ref, pscale_ref, wf_ref, bf_ref):
    gw = w_pool_ref.shape[1]
    half_scale = 0.5 * pscale_ref[...]
    wg = w_pool_ref[0]
    wf = jnp.dot(w_p_ref[...], wg, preferred_element_type=_F32, precision=lax.Precision.HIGHEST)
    wf_ref[...] = wf * half_scale
    b8 = jnp.broadcast_to(b_p_ref[...], (8, gw))
    bf = jnp.dot(b8, wg, preferred_element_type=_F32, precision=lax.Precision.HIGHEST)
    bf_ref[...] = bf[0:1] * half_scale


def _mix_kernel(x_ref, g_pre_ref, b_fold_ref, b_in_ref, ln_g_ref, ln_b_ref, ws_ref, bsp_ref, g_post_ref,
                w_fold_hbm, w_in_hbm, w_sgu_hbm, w_out_hbm, o_ref,
                w_fold_ref, w_in_ref, w_sgu_ref, w_out_ref, stage, sem,
                xn_scr, p_scr, s_scr, gv_scr, vn_scr, u_scr, gated_scr, m_scr, bb_scr, merged_scr, y_scr):
    t, d = x_ref.shape[1], x_ref.shape[2]
    n_groups = len(POOL_WINDOWS)
    gw = d // n_groups
    seq_idx = pl.program_id(1)
    x_tile = x_ref.at[0]
    o_tile = o_ref.at[0]

    @pl.when(jnp.logical_and(pl.program_id(0) == 0, seq_idx == 0))
    def _():
        jobs = _row_chunks(w_in_hbm.at[:, pl.ds(d, w_in_ref.shape[1])], w_in_ref)
        jobs += _row_chunks(w_fold_hbm, w_fold_ref) + _row_chunks(w_out_hbm, w_out_ref)
        for h in range(n_groups):
            jobs += _row_chunks(w_sgu_hbm.at[h], w_sgu_ref.at[h])
        _cast_weights_once(jobs, stage, sem)

    @pl.when(seq_idx == 0)
    def _():
        p_scr[0:HALO, :] = jnp.zeros((HALO, d), _F32)

    @pl.when(seq_idx > 0)
    def _():
        p_scr[0:HALO, :] = p_scr[t:t + HALO, :]

    for r in range(0, t, ROW_CHUNK):
        rows = slice(r, r + ROW_CHUNK)
        xn_scr[rows, :] = _rms_rows(x_tile[rows, :], g_pre_ref[...]).astype(_BF16)

    def proj(col0):
        return (_dot(xn_scr[...], w_in_ref[:, col0 - d:col0 - d + N_CHUNK])
                + b_in_ref[:, col0:col0 + N_CHUNK])

    for c in range(0, d, N_CHUNK):
        cs = slice(c, c + N_CHUNK)
        p_scr[HALO:HALO + t, cs] = _dot(xn_scr[...], w_fold_ref[:, cs]) + b_fold_ref[:, cs]

    row = lax.broadcasted_iota(jnp.int32, (HALO, gw), 0)
    pos1 = seq_idx * t + row + 1
    for gi, w in enumerate(POOL_WINDOWS):
        cols = slice(gi * gw, (gi + 1) * gw)
        src, k, lo, slot = p_scr.at[:, cols], 1, 8, 0
        while 2 * k < w:
            dst = s_scr.at[slot]
            dst[lo:HALO + t, :] = src[lo:HALO + t, :] + src[lo - k:HALO + t - k, :]
            src, k, lo, slot = dst, 2 * k, lo + 8, 1 - slot
        win = src[HALO:HALO + t, :] + src[HALO - k:HALO + t - k, :]
        cur = p_scr[HALO:HALO + t, cols]
        m_scr[:, cols] = win * (1.0 / w) - cur
        cnt = jnp.minimum(pos1, w).astype(_F32)
        m_scr[0:HALO, cols] = win[0:HALO] / cnt - cur[0:HALO]

    for c in range(0, d, N_CHUNK):
        gv_scr[:, c:c + N_CHUNK] = _gelu(proj(2 * d + c))
    for r in range(0, t, ROW_CHUNK):
        gv = gv_scr[r:r + ROW_CHUNK, :]
        mu = jnp.mean(gv, axis=-1, keepdims=True)
        xc = gv - mu
        var = jnp.mean(xc * xc, axis=-1, keepdims=True)
        vn = xc * lax.rsqrt(var + EPS) * ln_g_ref[...] + ln_b_ref[...]
        vn_scr[r:r + ROW_CHUNK, :] = vn.astype(_BF16)
    for c in range(0, d, N_CHUNK):
        u_scr[:, c:c + N_CHUNK] = _gelu(proj(d + c))

    qi = lax.broadcasted_iota(jnp.int32, (SGU_BLOCK, SGU_BLOCK), 0) // CHUNK
    kj = lax.broadcasted_iota(jnp.int32, (SGU_BLOCK, SGU_BLOCK), 1) // CHUNK
    causal = qi >= kj
    for h in range(n_groups):
        cols = slice(h * gw, (h + 1) * gw)
        half_ws = jnp.where(causal, 0.5 * ws_ref[h], 0.0).astype(_BF16)
        half_b = 0.5 * bsp_ref[h]
        for r in range(0, t, SGU_BLOCK):
            rows = slice(r, r + SGU_BLOCK)
            half_sv = _dot(half_ws, vn_scr[rows, cols]) + half_b
            gated_scr[rows, cols] = (u_scr[rows, cols] * half_sv).astype(_BF16)
    for h in range(n_groups):
        cols = slice(h * gw, (h + 1) * gw)
        bb_scr[:, cols] = _dot(gated_scr[:, cols], w_sgu_ref[h])

    for c in range(0, d, N_CHUNK):
        cs = slice(c, c + N_CHUNK)
        ta = jnp.tanh(0.5 * proj(3 * d + c))
        tb = jnp.tanh(0.5 * proj(4 * d + c))
        ha = m_scr[:, cs]
        hb = bb_scr[:, cs]
        merged_scr[:, cs] = ((ha + hb) + (ha * ta + hb * tb)).astype(_BF16)

    for c in range(0, d, N_CHUNK):
        y_scr[:, c:c + N_CHUNK] = _dot(merged_scr[...], w_out_ref[:, c:c + N_CHUNK])
    for r in range(0, t, ROW_CHUNK):
        rows = slice(r, r + ROW_CHUNK)
        o_tile[rows, :] = x_tile[rows, :] + _rms_rows(y_scr[rows, :], g_post_ref[...])


def _ffn_kernel(h_ref, g_pre_ref, g_post_ref, w1_hbm, w2_hbm, o_ref,
                w1_ref, w2_ref, stage, sem, hn_scr, f_scr, y_scr):
    t, d = h_ref.shape
    d_ff = w1_ref.shape[1]

    @pl.when(pl.program_id(0) == 0)
    def _():
        _cast_weights_once(_row_chunks(w1_hbm, w1_ref) + _row_chunks(w2_hbm, w2_ref), stage, sem)

    for r in range(0, t, ROW_CHUNK):
        rows = slice(r, r + ROW_CHUNK)
        hn_scr[rows, :] = _rms_rows(h_ref[rows, :], g_pre_ref[...]).astype(_BF16)
    for c in range(0, d_ff, N_CHUNK):
        f = jnp.maximum(_dot(hn_scr[...], w1_ref[:, c:c + N_CHUNK]), 0.0)
        f_scr[:, c:c + N_CHUNK] = (f * f).astype(_BF16)
    for c in range(0, d, N_CHUNK):
        y_scr[:, c:c + N_CHUNK] = _dot(f_scr[...], w2_ref[:, c:c + N_CHUNK])
    for r in range(0, t, ROW_CHUNK):
        rows = slice(r, r + ROW_CHUNK)
        o_ref[rows, :] = h_ref[rows, :] + _rms_rows(y_scr[rows, :], g_post_ref[...])


def _resident(shape):
    zeros = (0,) * len(shape)
    return pl.BlockSpec(shape, lambda *_: zeros, pipeline_mode=pl.Buffered(1))


_IN_HBM = pl.BlockSpec(memory_space=pltpu.HBM)


def kernel(x, norm1_pre_g, w_in, b_in, w_pool, pool_scale, sgu_ln_g, sgu_ln_b, w_spatial, b_spatial,
           w_sgu_proj, w_out, norm1_post_g, norm2_pre_g, w_ff1, w_ff2, norm2_post_g):
    bsz, s_len, d = x.shape
    d_in = w_in.shape[1]
    d_ff = w_ff1.shape[1]
    n_heads, blk, _ = w_spatial.shape
    gw = d // n_heads
    t = SEQ_TILE
    assert s_len % t == 0 and t % SGU_BLOCK == 0 and blk == SGU_BLOCK
    assert d % N_CHUNK == 0 and d_ff % N_CHUNK == 0 and d_in == 5 * d
    assert d % W_STAGE_ROWS == 0 and d_ff % W_STAGE_ROWS == 0 and gw % W_STAGE_ROWS == 0
    assert len(POOL_WINDOWS) == n_heads == w_pool.shape[0] and max(POOL_WINDOWS) <= HALO

    row = lambda v: v.reshape(1, -1).astype(_F32)
    w_in, w_out, w_sgu_proj = w_in.astype(_F32), w_out.astype(_F32), w_sgu_proj.astype(_F32)
    w_ff1, w_ff2 = w_ff1.astype(_F32), w_ff2.astype(_F32)
    b_in_row = row(b_in)
    stage_cols = max(d_in - d, d_ff)

    w_fold, b_fold = pl.pallas_call(
        _fold_pool_kernel,
        grid=(n_heads,),
        in_specs=[pl.BlockSpec((d, gw), lambda g: (0, g)), pl.BlockSpec((1, gw), lambda g: (0, g)),
                  pl.BlockSpec((1, gw, gw), lambda g: (g, 0, 0)), pl.BlockSpec((1, gw), lambda g: (0, g))],
        out_specs=[pl.BlockSpec((d, gw), lambda g: (0, g)), pl.BlockSpec((1, gw), lambda g: (0, g))],
        out_shape=[jax.ShapeDtypeStruct((d, d), _F32), jax.ShapeDtypeStruct((1, d), _F32)],
        name="fold_pool",
    )(w_in, b_in_row, w_pool.astype(_F32), row(pool_scale))

    bsp = jnp.broadcast_to(b_spatial.astype(_F32)[:, :, None], (n_heads, blk, gw))

    mix_vmem_in = (row(norm1_pre_g), b_fold, b_in_row, row(sgu_ln_g), row(sgu_ln_b),
                   w_spatial.astype(_F32), bsp, row(norm1_post_g))
    mix_hbm_in = (w_fold, w_in, w_sgu_proj, w_out)
    tile_spec = pl.BlockSpec((1, t, d), lambda b, s: (b, s, 0))
    h = pl.pallas_call(
        _mix_kernel,
        grid=(bsz, s_len // t),
        in_specs=[tile_spec] + [_resident(a.shape) for a in mix_vmem_in] + [_IN_HBM] * len(mix_hbm_in),
        out_specs=tile_spec,
        out_shape=jax.ShapeDtypeStruct(x.shape, x.dtype),
        scratch_shapes=[
            pltpu.VMEM((d, d), _BF16),
            pltpu.VMEM((d, d_in - d), _BF16),
            pltpu.VMEM(w_sgu_proj.shape, _BF16),
            pltpu.VMEM((d, d), _BF16),
            pltpu.VMEM((2, W_STAGE_ROWS, stage_cols), _F32),
            pltpu.SemaphoreType.DMA((2,)),
            pltpu.VMEM((t, d), _BF16),
            pltpu.VMEM((t + HALO, d), _F32),
            pltpu.VMEM((2, t + HALO, gw), _F32),
            pltpu.VMEM((t, d), _F32),
            pltpu.VMEM((t, d), _BF16),
            pltpu.VMEM((t, d), _F32),
            pltpu.VMEM((t, d), _BF16),
            pltpu.VMEM((t, d), _F32),
            pltpu.VMEM((t, d), _F32),
            pltpu.VMEM((t, d), _BF16),
            pltpu.VMEM((t, d), _F32),
        ],
        compiler_params=pltpu.CompilerParams(
            dimension_semantics=("arbitrary", "arbitrary"),
            vmem_limit_bytes=V7X_VMEM_LIMIT_BYTES),
        name="token_mix",
    )(x, *mix_vmem_in, *mix_hbm_in)

    m = bsz * s_len
    row_spec = pl.BlockSpec((t, d), lambda i: (i, 0))
    ffn_vmem_in = (row(norm2_pre_g), row(norm2_post_g))
    out = pl.pallas_call(
        _ffn_kernel,
        grid=(m // t,),
        in_specs=[row_spec] + [_resident(a.shape) for a in ffn_vmem_in] + [_IN_HBM] * 2,
        out_specs=row_spec,
        out_shape=jax.ShapeDtypeStruct((m, d), x.dtype),
        scratch_shapes=[
            pltpu.VMEM((d, d_ff), _BF16),
            pltpu.VMEM((d_ff, d), _BF16),
            pltpu.VMEM((2, W_STAGE_ROWS, stage_cols), _F32),
            pltpu.SemaphoreType.DMA((2,)),
            pltpu.VMEM((t, d), _BF16),
            pltpu.VMEM((t, d_ff), _BF16),
            pltpu.VMEM((t, d), _F32),
        ],
        compiler_params=pltpu.CompilerParams(
            dimension_semantics=("arbitrary",),
            vmem_limit_bytes=V7X_VMEM_LIMIT_BYTES),
        name="channel_mix",
    )(h.reshape(m, d), *ffn_vmem_in, w_ff1, w_ff2)
    return out.reshape(bsz, s_len, d)
```

```python
import math

import jax
import jax.numpy as jnp
from jax import lax
from jax.experimental import pallas as pl
from jax.experimental.pallas import tpu as pltpu

EPS = 1e-6
CHUNK = 64
POOL_WINDOWS = (2, 4, 8, 16)
SGU_BLOCK = 128
HALO = 32

SEQ_TILE = 512
N_CHUNK = 512
ROW_CHUNK = 32
W_STAGE_ROWS = 512
W_STAGE_COLS = 1024
W_STAGE_SLOTS = 3
V7X_VMEM_LIMIT_BYTES = 56 * 1024 * 1024

_GELU_C = math.sqrt(2.0 / math.pi)
_BF16 = jnp.bfloat16
_F32 = jnp.float32


def _gelu(x, scale=1.0):
    inner = x * (_GELU_C + (0.044715 * _GELU_C) * (x * x))
    hx = (0.5 * scale) * x
    return hx + hx * jnp.tanh(inner)


def _dot(a, b):
    return jnp.dot(a, b, preferred_element_type=_F32)


def _rms_rows(x, g):
    ms = jnp.mean(x * x, axis=-1, keepdims=True)
    return x * lax.rsqrt(ms + EPS) * g


def _weight_chunks(src_hbm, dst_vmem):
    n_rows, n_cols = dst_vmem.shape
    br, bc = min(n_rows, W_STAGE_ROWS), min(n_cols, W_STAGE_COLS)
    return [(src_hbm.at[pl.ds(r, br), pl.ds(c, bc)], dst_vmem.at[pl.ds(r, br), pl.ds(c, bc)])
            for r in range(0, n_rows, br) for c in range(0, n_cols, bc)]


def _cast_weights_once(jobs, stage, sem):
    def fetch(k):
        src, dst = jobs[k]
        slot = k % W_STAGE_SLOTS
        return pltpu.make_async_copy(src, stage.at[slot, pl.ds(0, dst.shape[0]), pl.ds(0, dst.shape[1])],
                                     sem.at[slot])

    ahead = W_STAGE_SLOTS - 1
    for k in range(min(ahead, len(jobs))):
        fetch(k).start()
    for k, (_, dst) in enumerate(jobs):
        if k + ahead < len(jobs):
            fetch(k + ahead).start()
        fetch(k).wait()
        dst[...] = stage[k % W_STAGE_SLOTS, 0:dst.shape[0], 0:dst.shape[1]].astype(_BF16)


def _fold_pool_kernel(w_p_ref, b_p_ref, w_pool_ref, pscale_ref, wf_ref, bf_ref):
    gw = w_pool_ref.shape[1]
    half_scale = 0.5 * pscale_ref[...]
    wg = w_pool_ref[0]
    wf = jnp.dot(w_p_ref[...], wg, preferred_element_type=_F32, precision=lax.Precision.HIGHEST)
    wf_ref[...] = wf * half_scale
    b8 = jnp.broadcast_to(b_p_ref[...], (8, gw))
    bf = jnp.dot(b8, wg, preferred_element_type=_F32, precision=lax.Precision.HIGHEST)
    bf_ref[...] = bf[0:1] * half_scale


def _mix_kernel(x_ref, g_pre_ref, b_fold_ref, b_in_ref, ln_g_ref, ln_b_ref, ws_ref, bsp_ref, g_post_ref,
                w_fold_hbm, w_in_hbm, w_sgu_hbm, w_out_hbm, o_ref,
                w_fold_ref, w_in_ref, w_sgu_ref, w_out_ref, stage, sem,
                xn_scr, p_scr, s_scr, gv_scr, vn_scr, u_scr, gated_scr, m_scr, bb_scr, merged_scr, y_scr):
    t, d = x_ref.shape[1], x_ref.shape[2]
    n_groups = len(POOL_WINDOWS)
    gw = d // n_groups
    seq_idx = pl.program_id(1)
    x_tile = x_ref.at[0]
    o_tile = o_ref.at[0]

    @pl.when(jnp.logical_and(pl.program_id(0) == 0, seq_idx == 0))
    def _():
        jobs = _weight_chunks(w_fold_hbm, w_fold_ref)
        jobs += _weight_chunks(w_in_hbm.at[:, pl.ds(d, w_in_ref.shape[1])], w_in_ref)
        jobs += _weight_chunks(w_sgu_hbm, w_sgu_ref) + _weight_chunks(w_out_hbm, w_out_ref)
        _cast_weights_once(jobs, stage, sem)

    @pl.when(seq_idx == 0)
    def _():
        p_scr[0:HALO, :] = jnp.zeros((HALO, d), _F32)

    @pl.when(seq_idx > 0)
    def _():
        p_scr[0:HALO, :] = p_scr[t:t + HALO, :]

    for r in range(0, t, ROW_CHUNK):
        rows = slice(r, r + ROW_CHUNK)
        xn_scr[rows, :] = _rms_rows(x_tile[rows, :], g_pre_ref[...]).astype(_BF16)

    def proj(col0):
        return (_dot(xn_scr[...], w_in_ref[:, col0 - d:col0 - d + N_CHUNK])
                + b_in_ref[:, col0:col0 + N_CHUNK])

    for c in range(0, d, N_CHUNK):
        cs = slice(c, c + N_CHUNK)
        p_scr[HALO:HALO + t, cs] = _dot(xn_scr[...], w_fold_ref[:, cs]) + b_fold_ref[:, cs]

    row = lax.broadcasted_iota(jnp.int32, (HALO, gw), 0)
    pos1 = seq_idx * t + row + 1
    for gi, w in enumerate(POOL_WINDOWS):
        cols = slice(gi * gw, (gi + 1) * gw)
        src, k, lo, slot = p_scr.at[:, cols], 1, 8, 0
        while 2 * k < w:
            dst = s_scr.at[slot]
            dst[lo:HALO + t, :] = src[lo:HALO + t, :] + src[lo - k:HALO + t - k, :]
            src, k, lo, slot = dst, 2 * k, lo + 8, 1 - slot
        win = src[HALO:HALO + t, :] + src[HALO - k:HALO + t - k, :]
        cur = p_scr[HALO:HALO + t, cols]
        m_scr[:, cols] = win * (1.0 / w) - cur
        cnt = jnp.minimum(pos1, w).astype(_F32)
        m_scr[0:HALO, cols] = win[0:HALO] / cnt - cur[0:HALO]

    for c in range(0, d, N_CHUNK):
        gv_scr[:, c:c + N_CHUNK] = _gelu(proj(2 * d + c))
    for r in range(0, t, ROW_CHUNK):
        gv = gv_scr[r:r + ROW_CHUNK, :]
        mu = jnp.mean(gv, axis=-1, keepdims=True)
        xc = gv - mu
        var = jnp.mean(xc * xc, axis=-1, keepdims=True)
        vn = xc * lax.rsqrt(var + EPS) * ln_g_ref[...] + ln_b_ref[...]
        vn_scr[r:r + ROW_CHUNK, :] = vn.astype(_BF16)
    for c in range(0, d, N_CHUNK):
        u_scr[:, c:c + N_CHUNK] = _gelu(proj(d + c), scale=0.5)

    qi = lax.broadcasted_iota(jnp.int32, (SGU_BLOCK, SGU_BLOCK), 0) // CHUNK
    kj = lax.broadcasted_iota(jnp.int32, (SGU_BLOCK, SGU_BLOCK), 1) // CHUNK
    causal = qi >= kj
    for h in range(n_groups):
        cols = slice(h * gw, (h + 1) * gw)
        ws = jnp.where(causal, ws_ref[h], 0.0).astype(_BF16)
        for r in range(0, t, SGU_BLOCK):
            rows = slice(r, r + SGU_BLOCK)
            sv = _dot(ws, vn_scr[rows, cols]) + bsp_ref[h]
            gated_scr[rows, cols] = (u_scr[rows, cols] * sv).astype(_BF16)
    for h in range(n_groups):
        cols = slice(h * gw, (h + 1) * gw)
        bb_scr[:, cols] = _dot(gated_scr[:, cols], w_sgu_ref[cols, :])

    for c in range(0, d, N_CHUNK):
        cs = slice(c, c + N_CHUNK)
        ta = jnp.tanh(0.5 * proj(3 * d + c))
        tb = jnp.tanh(0.5 * proj(4 * d + c))
        ha = m_scr[:, cs]
        hb = bb_scr[:, cs]
        merged_scr[:, cs] = ((ha + hb) + (ha * ta + hb * tb)).astype(_BF16)

    for c in range(0, d, N_CHUNK):
        y_scr[:, c:c + N_CHUNK] = _dot(merged_scr[...], w_out_ref[:, c:c + N_CHUNK])
    for r in range(0, t, ROW_CHUNK):
        rows = slice(r, r + ROW_CHUNK)
        o_tile[rows, :] = x_tile[rows, :] + _rms_rows(y_scr[rows, :], g_post_ref[...])


def _ffn_kernel(h_ref, g_pre_ref, g_post_ref, w1_hbm, w2_hbm, o_ref,
                w1_ref, w2_ref, stage, sem, hn_scr, f_scr, y_scr):
    t, d = h_ref.shape
    d_ff = w1_ref.shape[1]

    @pl.when(pl.program_id(0) == 0)
    def _():
        _cast_weights_once(_weight_chunks(w1_hbm, w1_ref) + _weight_chunks(w2_hbm, w2_ref), stage, sem)

    for r in range(0, t, ROW_CHUNK):
        rows = slice(r, r + ROW_CHUNK)
        hn_scr[rows, :] = _rms_rows(h_ref[rows, :], g_pre_ref[...]).astype(_BF16)
    for c in range(0, d_ff, N_CHUNK):
        f = jnp.maximum(_dot(hn_scr[...], w1_ref[:, c:c + N_CHUNK]), 0.0)
        f_scr[:, c:c + N_CHUNK] = (f * f).astype(_BF16)
    for c in range(0, d, N_CHUNK):
        y_scr[:, c:c + N_CHUNK] = _dot(f_scr[...], w2_ref[:, c:c + N_CHUNK])
    for r in range(0, t, ROW_CHUNK):
        rows = slice(r, r + ROW_CHUNK)
        o_ref[rows, :] = h_ref[rows, :] + _rms_rows(y_scr[rows, :], g_post_ref[...])


def _resident(shape):
    zeros = (0,) * len(shape)
    return pl.BlockSpec(shape, lambda *_: zeros, pipeline_mode=pl.Buffered(1))


_IN_HBM = pl.BlockSpec(memory_space=pltpu.HBM)


def kernel(x, norm1_pre_g, w_in, b_in, w_pool, pool_scale, sgu_ln_g, sgu_ln_b, w_spatial, b_spatial,
           w_sgu_proj, w_out, norm1_post_g, norm2_pre_g, w_ff1, w_ff2, norm2_post_g):
    bsz, s_len, d = x.shape
    d_in = w_in.shape[1]
    d_ff = w_ff1.shape[1]
    n_heads, blk, _ = w_spatial.shape
    gw = d // n_heads
    t = SEQ_TILE
    assert s_len % t == 0 and t % SGU_BLOCK == 0 and blk == SGU_BLOCK
    assert d % N_CHUNK == 0 and d_ff % N_CHUNK == 0 and d_in == 5 * d
    assert d % W_STAGE_ROWS == 0 and d_ff % W_STAGE_ROWS == 0
    assert d % W_STAGE_COLS == 0 and d_ff % W_STAGE_COLS == 0 and gw <= W_STAGE_COLS
    assert len(POOL_WINDOWS) == n_heads == w_pool.shape[0] and max(POOL_WINDOWS) <= HALO

    row = lambda v: v.reshape(1, -1).astype(_F32)
    w_in, w_out, w_ff1, w_ff2 = w_in.astype(_F32), w_out.astype(_F32), w_ff1.astype(_F32), w_ff2.astype(_F32)
    w_sgu_rows = w_sgu_proj.astype(_F32).reshape(n_heads * gw, gw)
    b_in_row = row(b_in)
    stage_shape = (W_STAGE_SLOTS, W_STAGE_ROWS, W_STAGE_COLS)

    w_fold, b_fold = pl.pallas_call(
        _fold_pool_kernel,
        grid=(n_heads,),
        in_specs=[pl.BlockSpec((d, gw), lambda g: (0, g)), pl.BlockSpec((1, gw), lambda g: (0, g)),
                  pl.BlockSpec((1, gw, gw), lambda g: (g, 0, 0)), pl.BlockSpec((1, gw), lambda g: (0, g))],
        out_specs=[pl.BlockSpec((d, gw), lambda g: (0, g)), pl.BlockSpec((1, gw), lambda g: (0, g))],
        out_shape=[jax.ShapeDtypeStruct((d, d), _F32), jax.ShapeDtypeStruct((1, d), _F32)],
        name="fold_pool",
    )(w_in, b_in_row, w_pool.astype(_F32), row(pool_scale))

    bsp = jnp.broadcast_to(b_spatial.astype(_F32)[:, :, None], (n_heads, blk, gw))

    mix_vmem_in = (row(norm1_pre_g), b_fold, b_in_row, row(sgu_ln_g), row(sgu_ln_b),
                   w_spatial.astype(_F32), bsp, row(norm1_post_g))
    mix_hbm_in = (w_fold, w_in, w_sgu_rows, w_out)
    tile_spec = pl.BlockSpec((1, t, d), lambda b, s: (b, s, 0))
    h = pl.pallas_call(
        _mix_kernel,
        grid=(bsz, s_len // t),
        in_specs=[tile_spec] + [_resident(a.shape) for a in mix_vmem_in] + [_IN_HBM] * len(mix_hbm_in),
        out_specs=tile_spec,
        out_shape=jax.ShapeDtypeStruct(x.shape, x.dtype),
        scratch_shapes=[
            pltpu.VMEM((d, d), _BF16),
            pltpu.VMEM((d, d_in - d), _BF16),
            pltpu.VMEM(w_sgu_rows.shape, _BF16),
            pltpu.VMEM((d, d), _BF16),
            pltpu.VMEM(stage_shape, _F32),
            pltpu.SemaphoreType.DMA((W_STAGE_SLOTS,)),
            pltpu.VMEM((t, d), _BF16),
            pltpu.VMEM((t + HALO, d), _F32),
            pltpu.VMEM((2, t + HALO, gw), _F32),
            pltpu.VMEM((t, d), _F32),
            pltpu.VMEM((t, d), _BF16),
            pltpu.VMEM((t, d), _F32),
            pltpu.VMEM((t, d), _BF16),
            pltpu.VMEM((t, d), _F32),
            pltpu.VMEM((t, d), _F32),
            pltpu.VMEM((t, d), _BF16),
            pltpu.VMEM((t, d), _F32),
        ],
        compiler_params=pltpu.CompilerParams(
            dimension_semantics=("arbitrary", "arbitrary"),
            vmem_limit_bytes=V7X_VMEM_LIMIT_BYTES),
        name="token_mix",
    )(x, *mix_vmem_in, *mix_hbm_in)

    m = bsz * s_len
    row_spec = pl.BlockSpec((t, d), lambda i: (i, 0))
    ffn_vmem_in = (row(norm2_pre_g), row(norm2_post_g))
    out = pl.pallas_call(
        _ffn_kernel,
        grid=(m // t,),
        in_specs=[row_spec] + [_resident(a.shape) for a in ffn_vmem_in] + [_IN_HBM] * 2,
        out_specs=row_spec,
        out_shape=jax.ShapeDtypeStruct((m, d), x.dtype),
        scratch_shapes=[
            pltpu.VMEM((d, d_ff), _BF16),
            pltpu.VMEM((d_ff, d), _BF16),
            pltpu.VMEM(stage_shape, _F32),
            pltpu.SemaphoreType.DMA((W_STAGE_SLOTS,)),
            pltpu.VMEM((t, d), _BF16),
            pltpu.VMEM((t, d_ff), _BF16),
            pltpu.VMEM((t, d), _F32),
        ],
        compiler_params=pltpu.CompilerParams(
            dimension_semantics=("arbitrary",),
            vmem_limit_bytes=V7X_VMEM_LIMIT_BYTES),
        name="channel_mix",
    )(h.reshape(m, d), *ffn_vmem_in, w_ff1, w_ff2)
    return out.reshape(bsz, s_len, d)
```

```python
import math

import jax
import jax.numpy as jnp
from jax import lax
from jax.experimental import pallas as pl
from jax.experimental.pallas import tpu as pltpu

EPS = 1e-6
CHUNK = 64
POOL_WINDOWS = (2, 4, 8, 16)
SGU_BLOCK = 128
HALO = 32

SEQ_TILE = 512
N_CHUNK = 512
ROW_CHUNK = 32
W_STAGE_ROWS = 512
W_STAGE_COLS = 1024
W_STAGE_SLOTS = 3
V7X_VMEM_LIMIT_BYTES = 56 * 1024 * 1024

_GELU_C = math.sqrt(2.0 / math.pi)
_BF16 = jnp.bfloat16
_F32 = jnp.float32


def _gelu(x, scale=1.0):
    inner = x * (_GELU_C + (0.044715 * _GELU_C) * (x * x))
    hx = (0.5 * scale) * x
    return hx + hx * jnp.tanh(inner)


def _dot(a, b):
    return jnp.dot(a, b, preferred_element_type=_F32)


def _dot_exact(a, b):
    a_hi, b_hi = a.astype(_BF16), b.astype(_BF16)
    a_lo = (a - a_hi.astype(_F32)).astype(_BF16)
    b_lo = (b - b_hi.astype(_F32)).astype(_BF16)
    return _dot(a_hi, b_hi) + (_dot(a_hi, b_lo) + _dot(a_lo, b_hi))


def _pre_norm(src_ref, dst_ref):
    for r in range(0, src_ref.shape[0], ROW_CHUNK):
        rows = slice(r, r + ROW_CHUNK)
        x = src_ref[rows, :]
        ms = jnp.mean(x * x, axis=-1, keepdims=True)
        dst_ref[rows, :] = (x * lax.rsqrt(ms + EPS)).astype(_BF16)


def _project_norm_residual(lhs_ref, w_ref, y_scr, res_ref, g_ref, o_ref):
    t, d = y_scr.shape
    for c in range(0, d, N_CHUNK):
        y_scr[:, c:c + N_CHUNK] = _dot(lhs_ref[...], w_ref[:, c:c + N_CHUNK])
    for r in range(0, t, ROW_CHUNK):
        rows = slice(r, r + ROW_CHUNK)
        y = y_scr[rows, :]
        ms = jnp.mean(y * y, axis=-1, keepdims=True)
        o_ref[rows, :] = res_ref[rows, :] + y * lax.rsqrt(ms + EPS) * g_ref[...]


def _cast_jobs(src_hbm, dst_vmem, row_gain_ref=None, col_block_scale=None):
    n_rows, n_cols = dst_vmem.shape
    br, bc = min(n_rows, W_STAGE_ROWS), min(n_cols, W_STAGE_COLS)
    jobs = []
    for r in range(0, n_rows, br):
        for c in range(0, n_cols, bc):
            def finish(stage_view, r=r, c=c):
                w = stage_view[...]
                if row_gain_ref is not None:
                    w = w * row_gain_ref[r:r + br, :]
                if col_block_scale is not None and col_block_scale(c) != 1.0:
                    w = w * col_block_scale(c)
                dst_vmem[r:r + br, c:c + bc] = w.astype(_BF16)
            jobs.append((src_hbm.at[pl.ds(r, br), pl.ds(c, bc)], (br, bc), finish))
    return jobs


def _run_weight_jobs(jobs, stage, sem):
    def fetch(k):
        src, (br, bc), _ = jobs[k]
        slot = k % W_STAGE_SLOTS
        return pltpu.make_async_copy(src, stage.at[slot, pl.ds(0, br), pl.ds(0, bc)], sem.at[slot])

    ahead = W_STAGE_SLOTS - 1
    for k in range(min(ahead, len(jobs))):
        fetch(k).start()
    for k, (_, (br, bc), finish) in enumerate(jobs):
        if k + ahead < len(jobs):
            fetch(k + ahead).start()
        fetch(k).wait()
        finish(stage.at[k % W_STAGE_SLOTS, pl.ds(0, br), pl.ds(0, bc)])


def _mix_kernel(x_ref, g_pre_col_ref, b_in_ref, w_pool_ref, pscale_ref, ln_g_ref, ln_b_ref, ws_ref, bsp_ref,
                g_post_ref, w_in_hbm, w_sgu_hbm, w_out_hbm, o_ref,
                w_fold_ref, b_fold_ref, w_in_ref, w_sgu_ref, w_out_ref, stage, sem,
                xn_scr, p_scr, s_scr, gv_scr, vn_scr, u_scr, gated_scr, m_scr, bb_scr, merged_scr, y_scr):
    t, d = x_ref.shape[1], x_ref.shape[2]
    n_groups = len(POOL_WINDOWS)
    gw = d // n_groups
    seq_idx = pl.program_id(1)
    x_tile = x_ref.at[0]

    @pl.when(jnp.logical_and(pl.program_id(0) == 0, seq_idx == 0))
    def _():
        def fold_rows(stage_view, r):
            gain = g_pre_col_ref[r:r + W_STAGE_ROWS, :]
            for gi in range(n_groups):
                cols = slice(gi * gw, (gi + 1) * gw)
                wf = _dot_exact(stage_view[:, cols] * gain, w_pool_ref[gi]) * (0.5 * pscale_ref[:, cols])
                w_fold_ref[r:r + W_STAGE_ROWS, cols] = wf.astype(_BF16)

        jobs = [(w_in_hbm.at[pl.ds(r, W_STAGE_ROWS), pl.ds(0, d)], (W_STAGE_ROWS, d),
                 lambda view, r=r: fold_rows(view, r)) for r in range(0, d, W_STAGE_ROWS)]
        jobs += _cast_jobs(w_in_hbm.at[:, pl.ds(d, w_in_ref.shape[1])], w_in_ref, row_gain_ref=g_pre_col_ref,
                           col_block_scale=lambda c: 0.5 if c >= 2 * d else 1.0)
        jobs += _cast_jobs(w_sgu_hbm, w_sgu_ref) + _cast_jobs(w_out_hbm, w_out_ref)
        _run_weight_jobs(jobs, stage, sem)
        for gi in range(n_groups):
            cols = slice(gi * gw, (gi + 1) * gw)
            b8 = jnp.broadcast_to(b_in_ref[:, cols], (8, gw))
            b_fold_ref[:, cols] = _dot_exact(b8, w_pool_ref[gi])[0:1] * (0.5 * pscale_ref[:, cols])

    @pl.when(seq_idx == 0)
    def _():
        p_scr[0:HALO, :] = jnp.zeros((HALO, d), _F32)

    @pl.when(seq_idx > 0)
    def _():
        p_scr[0:HALO, :] = p_scr[t:t + HALO, :]

    _pre_norm(x_tile, xn_scr)

    def proj(col0, scale=1.0):
        bias = b_in_ref[:, col0:col0 + N_CHUNK]
        return _dot(xn_scr[...], w_in_ref[:, col0 - d:col0 - d + N_CHUNK]) + (bias if scale == 1.0 else scale * bias)

    for c in range(0, d, N_CHUNK):
        cs = slice(c, c + N_CHUNK)
        p_scr[HALO:HALO + t, cs] = _dot(xn_scr[...], w_fold_ref[:, cs]) + b_fold_ref[:, cs]

    row = lax.broadcasted_iota(jnp.int32, (HALO, gw), 0)
    pos1 = seq_idx * t + row + 1
    for gi, w in enumerate(POOL_WINDOWS):
        cols = slice(gi * gw, (gi + 1) * gw)
        src, k, lo, slot = p_scr.at[:, cols], 1, 8, 0
        while 2 * k < w:
            dst = s_scr.at[slot]
            dst[lo:HALO + t, :] = src[lo:HALO + t, :] + src[lo - k:HALO + t - k, :]
            src, k, lo, slot = dst, 2 * k, lo + 8, 1 - slot
        win = src[HALO:HALO + t, :] + src[HALO - k:HALO + t - k, :]
        cur = p_scr[HALO:HALO + t, cols]
        m_scr[:, cols] = win * (1.0 / w) - cur
        cnt = jnp.minimum(pos1, w).astype(_F32)
        m_scr[0:HALO, cols] = win[0:HALO] / cnt - cur[0:HALO]

    for c in range(0, d, N_CHUNK):
        gv_scr[:, c:c + N_CHUNK] = _gelu(proj(2 * d + c))
    for r in range(0, t, ROW_CHUNK):
        gv = gv_scr[r:r + ROW_CHUNK, :]
        mu = jnp.mean(gv, axis=-1, keepdims=True)
        xc = gv - mu
        var = jnp.mean(xc * xc, axis=-1, keepdims=True)
        vn = xc * lax.rsqrt(var + EPS) * ln_g_ref[...] + ln_b_ref[...]
        vn_scr[r:r + ROW_CHUNK, :] = vn.astype(_BF16)
    for c in range(0, d, N_CHUNK):
        u_scr[:, c:c + N_CHUNK] = _gelu(proj(d + c), scale=0.5)

    qi = lax.broadcasted_iota(jnp.int32, (SGU_BLOCK, SGU_BLOCK), 0) // CHUNK
    kj = lax.broadcasted_iota(jnp.int32, (SGU_BLOCK, SGU_BLOCK), 1) // CHUNK
    causal = qi >= kj
    for h in range(n_groups):
        cols = slice(h * gw, (h + 1) * gw)
        ws = jnp.where(causal, ws_ref[h], 0.0).astype(_BF16)
        for r in range(0, t, SGU_BLOCK):
            rows = slice(r, r + SGU_BLOCK)
            sv = _dot(ws, vn_scr[rows, cols]) + bsp_ref[h]
            gated_scr[rows, cols] = (u_scr[rows, cols] * sv).astype(_BF16)
    for h in range(n_groups):
        cols = slice(h * gw, (h + 1) * gw)
        bb_scr[:, cols] = _dot(gated_scr[:, cols], w_sgu_ref[cols, :])

    for c in range(0, d, N_CHUNK):
        cs = slice(c, c + N_CHUNK)
        ta = jnp.tanh(proj(3 * d + c, scale=0.5))
        tb = jnp.tanh(proj(4 * d + c, scale=0.5))
        ha = m_scr[:, cs]
        hb = bb_scr[:, cs]
        merged_scr[:, cs] = ((ha + hb) + (ha * ta + hb * tb)).astype(_BF16)

    _project_norm_residual(merged_scr, w_out_ref, y_scr, x_tile, g_post_ref, o_ref.at[0])


def _ffn_kernel(h_ref, g_pre_col_ref, g_post_ref, w1_hbm, w2_hbm, o_ref,
                w1_ref, w2_ref, stage, sem, hn_scr, f_scr, y_scr):
    d_ff = w1_ref.shape[1]

    @pl.when(pl.program_id(0) == 0)
    def _():
        jobs = _cast_jobs(w1_hbm, w1_ref, row_gain_ref=g_pre_col_ref) + _cast_jobs(w2_hbm, w2_ref)
        _run_weight_jobs(jobs, stage, sem)

    _pre_norm(h_ref, hn_scr)
    for c in range(0, d_ff, N_CHUNK):
        f = jnp.maximum(_dot(hn_scr[...], w1_ref[:, c:c + N_CHUNK]), 0.0)
        f_scr[:, c:c + N_CHUNK] = (f * f).astype(_BF16)
    _project_norm_residual(f_scr, w2_ref, y_scr, h_ref, g_post_ref, o_ref)


def _resident(shape):
    zeros = (0,) * len(shape)
    return pl.BlockSpec(shape, lambda *_: zeros, pipeline_mode=pl.Buffered(1))


_IN_HBM = pl.BlockSpec(memory_space=pltpu.HBM)


def kernel(x, norm1_pre_g, w_in, b_in, w_pool, pool_scale, sgu_ln_g, sgu_ln_b, w_spatial, b_spatial,
           w_sgu_proj, w_out, norm1_post_g, norm2_pre_g, w_ff1, w_ff2, norm2_post_g):
    bsz, s_len, d = x.shape
    d_in = w_in.shape[1]
    d_ff = w_ff1.shape[1]
    n_heads, blk, _ = w_spatial.shape
    gw = d // n_heads
    t = SEQ_TILE
    assert s_len % t == 0 and t % SGU_BLOCK == 0 and blk == SGU_BLOCK
    assert d % N_CHUNK == 0 and d_ff % N_CHUNK == 0 and d_in == 5 * d
    assert d % W_STAGE_ROWS == 0 and d_ff % W_STAGE_ROWS == 0
    assert d == W_STAGE_COLS and d_ff % W_STAGE_COLS == 0 and gw <= W_STAGE_COLS
    assert len(POOL_WINDOWS) == n_heads == w_pool.shape[0] and max(POOL_WINDOWS) <= HALO

    row = lambda v: v.reshape(1, -1).astype(_F32)
    col = lambda v: v.reshape(-1, 1).astype(_F32)
    w_in, w_out, w_ff1, w_ff2 = w_in.astype(_F32), w_out.astype(_F32), w_ff1.astype(_F32), w_ff2.astype(_F32)
    w_sgu_rows = w_sgu_proj.astype(_F32).reshape(n_heads * gw, gw)
    stage_shape = (W_STAGE_SLOTS, W_STAGE_ROWS, W_STAGE_COLS)

    bsp = jnp.broadcast_to(b_spatial.astype(_F32)[:, :, None], (n_heads, blk, gw))

    mix_vmem_in = (col(norm1_pre_g), row(b_in), w_pool.astype(_F32), row(pool_scale), row(sgu_ln_g),
                   row(sgu_ln_b), w_spatial.astype(_F32), bsp, row(norm1_post_g))
    mix_hbm_in = (w_in, w_sgu_rows, w_out)
    tile_spec = pl.BlockSpec((1, t, d), lambda b, s: (b, s, 0))
    h = pl.pallas_call(
        _mix_kernel,
        grid=(bsz, s_len // t),
        in_specs=[tile_spec] + [_resident(a.shape) for a in mix_vmem_in] + [_IN_HBM] * len(mix_hbm_in),
        out_specs=tile_spec,
        out_shape=jax.ShapeDtypeStruct(x.shape, x.dtype),
        scratch_shapes=[
            pltpu.VMEM((d, d), _BF16),
            pltpu.VMEM((1, d), _F32),
            pltpu.VMEM((d, d_in - d), _BF16),
            pltpu.VMEM(w_sgu_rows.shape, _BF16),
            pltpu.VMEM((d, d), _BF16),
            pltpu.VMEM(stage_shape, _F32),
            pltpu.SemaphoreType.DMA((W_STAGE_SLOTS,)),
            pltpu.VMEM((t, d), _BF16),
            pltpu.VMEM((t + HALO, d), _F32),
            pltpu.VMEM((2, t + HALO, gw), _F32),
            pltpu.VMEM((t, d), _F32),
            pltpu.VMEM((t, d), _BF16),
            pltpu.VMEM((t, d), _F32),
            pltpu.VMEM((t, d), _BF16),
            pltpu.VMEM((t, d), _F32),
            pltpu.VMEM((t, d), _F32),
            pltpu.VMEM((t, d), _BF16),
            pltpu.VMEM((t, d), _F32),
        ],
        compiler_params=pltpu.CompilerParams(
            dimension_semantics=("arbitrary", "arbitrary"),
            vmem_limit_bytes=V7X_VMEM_LIMIT_BYTES),
        name="token_mix",
    )(x, *mix_vmem_in, *mix_hbm_in)

    m = bsz * s_len
    row_spec = pl.BlockSpec((t, d), lambda i: (i, 0))
    ffn_vmem_in = (col(norm2_pre_g), row(norm2_post_g))
    out = pl.pallas_call(
        _ffn_kernel,
        grid=(m // t,),
        in_specs=[row_spec] + [_resident(a.shape) for a in ffn_vmem_in] + [_IN_HBM] * 2,
        out_specs=row_spec,
        out_shape=jax.ShapeDtypeStruct((m, d), x.dtype),
        scratch_shapes=[
            pltpu.VMEM((d, d_ff), _BF16),
            pltpu.VMEM((d_ff, d), _BF16),
            pltpu.VMEM(stage_shape, _F32),
            pltpu.SemaphoreType.DMA((W_STAGE_SLOTS,)),
            pltpu.VMEM((t, d), _BF16),
            pltpu.VMEM((t, d_ff), _BF16),
            pltpu.VMEM((t, d), _F32),
        ],
        compiler_params=pltpu.CompilerParams(
            dimension_semantics=("arbitrary",),
            vmem_limit_bytes=V7X_VMEM_LIMIT_BYTES),
        name="channel_mix",
    )(h.reshape(m, d), *ffn_vmem_in, w_ff1, w_ff2)
    return out.reshape(bsz, s_len, d)
```

```python
import functools
import math

import jax
import jax.numpy as jnp
from jax import lax
from jax.experimental import pallas as pl
from jax.experimental.pallas import tpu as pltpu

EPS = 1e-6
CHUNK = 64
POOL_WINDOWS = (2, 4, 8, 16)
SGU_BLOCK = 128
HALO = 32

SEQ_TILE = 512
N_CHUNK = 512
ROW_CHUNK = 32
W_STAGE_ROWS = 512
W_STAGE_COLS = 1024
W_STAGE_SLOTS = 3
V7X_VMEM_LIMIT_BYTES = 56 * 1024 * 1024

_GELU_C = math.sqrt(2.0 / math.pi)
_BF16 = jnp.bfloat16
_F32 = jnp.float32


def _gelu(x, scale=1.0):
    inner = x * (_GELU_C + (0.044715 * _GELU_C) * (x * x))
    hx = (0.5 * scale) * x
    return hx + hx * jnp.tanh(inner)


def _dot(a, b):
    return jnp.dot(a, b, preferred_element_type=_F32)


def _dot_exact(a, b):
    a_hi, b_hi = a.astype(_BF16), b.astype(_BF16)
    a_lo = (a - a_hi.astype(_F32)).astype(_BF16)
    b_lo = (b - b_hi.astype(_F32)).astype(_BF16)
    return _dot(a_hi, b_hi) + (_dot(a_hi, b_lo) + _dot(a_lo, b_hi))


def _pre_norm(src_ref, dst_ref):
    for r in range(0, src_ref.shape[0], ROW_CHUNK):
        rows = slice(r, r + ROW_CHUNK)
        x = src_ref[rows, :]
        ms = jnp.mean(x * x, axis=-1, keepdims=True)
        dst_ref[rows, :] = (x * lax.rsqrt(ms + EPS)).astype(_BF16)


def _project_norm_residual(lhs_ref, w_ref, y_scr, res_ref, g_ref, o_ref):
    t, d = y_scr.shape
    for c in range(0, d, N_CHUNK):
        y_scr[:, c:c + N_CHUNK] = _dot(lhs_ref[...], w_ref[:, c:c + N_CHUNK])
    for r in range(0, t, ROW_CHUNK):
        rows = slice(r, r + ROW_CHUNK)
        y = y_scr[rows, :]
        ms = jnp.mean(y * y, axis=-1, keepdims=True)
        o_ref[rows, :] = res_ref[rows, :] + y * lax.rsqrt(ms + EPS) * g_ref[...]


def _cast_jobs(src_hbm, dst_vmem, row_gain_ref=None, col_block_scale=None):
    n_rows, n_cols = dst_vmem.shape
    br, bc = min(n_rows, W_STAGE_ROWS), min(n_cols, W_STAGE_COLS)
    jobs = []
    for r in range(0, n_rows, br):
        for c in range(0, n_cols, bc):
            def finish(stage_view, r=r, c=c):
                w = stage_view[...]
                if row_gain_ref is not None:
                    w = w * row_gain_ref[r:r + br, :]
                if col_block_scale is not None and col_block_scale(c) != 1.0:
                    w = w * col_block_scale(c)
                dst_vmem[r:r + br, c:c + bc] = w.astype(_BF16)
            jobs.append((src_hbm.at[pl.ds(r, br), pl.ds(c, bc)], (br, bc), finish))
    return jobs


def _run_weight_jobs(jobs, stage, sem):
    def fetch(k):
        src, (br, bc), _ = jobs[k]
        slot = k % W_STAGE_SLOTS
        return pltpu.make_async_copy(src, stage.at[slot, pl.ds(0, br), pl.ds(0, bc)], sem.at[slot])

    ahead = W_STAGE_SLOTS - 1
    for k in range(min(ahead, len(jobs))):
        fetch(k).start()
    for k, (_, (br, bc), finish) in enumerate(jobs):
        if k + ahead < len(jobs):
            fetch(k + ahead).start()
        fetch(k).wait()
        finish(stage.at[k % W_STAGE_SLOTS, pl.ds(0, br), pl.ds(0, bc)])


def _mix_kernel(x_ref, g_pre_col_ref, b_in_ref, w_pool_ref, pscale_ref, ln_g_ref, ln_b_ref, ws_ref, bsp_ref,
                g_post_ref, w_in_hbm, w_sgu_hbm, w_out_hbm, o_ref,
                w_fold_ref, b_fold_ref, w_in_ref, w_sgu_ref, w_out_ref, stage, sem,
                xn_scr, p_scr, s_scr, gv_scr, vn_scr, u_scr, gated_scr, m_scr, bb_scr, merged_scr, y_scr):
    t, d = x_ref.shape[1], x_ref.shape[2]
    n_groups = len(POOL_WINDOWS)
    gw = d // n_groups
    seq_idx = pl.program_id(1)
    x_tile = x_ref.at[0]

    @pl.when(jnp.logical_and(pl.program_id(0) == 0, seq_idx == 0))
    def _():
        def fold_rows(stage_view, r):
            gain = g_pre_col_ref[r:r + W_STAGE_ROWS, :]
            for gi in range(n_groups):
                cols = slice(gi * gw, (gi + 1) * gw)
                wf = _dot_exact(stage_view[:, cols] * gain, w_pool_ref[gi]) * (0.5 * pscale_ref[:, cols])
                w_fold_ref[r:r + W_STAGE_ROWS, cols] = wf.astype(_BF16)

        jobs = [(w_in_hbm.at[pl.ds(r, W_STAGE_ROWS), pl.ds(0, d)], (W_STAGE_ROWS, d),
                 lambda view, r=r: fold_rows(view, r)) for r in range(0, d, W_STAGE_ROWS)]
        jobs += _cast_jobs(w_in_hbm.at[:, pl.ds(d, w_in_ref.shape[1])], w_in_ref, row_gain_ref=g_pre_col_ref,
                           col_block_scale=lambda c: 0.5 if c >= 2 * d else 1.0)
        jobs += _cast_jobs(w_sgu_hbm, w_sgu_ref) + _cast_jobs(w_out_hbm, w_out_ref)
        _run_weight_jobs(jobs, stage, sem)
        for gi in range(n_groups):
            cols = slice(gi * gw, (gi + 1) * gw)
            b8 = jnp.broadcast_to(b_in_ref[:, cols], (8, gw))
            b_fold_ref[:, cols] = _dot_exact(b8, w_pool_ref[gi])[0:1] * (0.5 * pscale_ref[:, cols])

    @pl.when(seq_idx == 0)
    def _():
        p_scr[0:HALO, :] = jnp.zeros((HALO, d), _F32)

    @pl.when(seq_idx > 0)
    def _():
        p_scr[0:HALO, :] = p_scr[t:t + HALO, :]

    _pre_norm(x_tile, xn_scr)

    def proj(col0, scale=1.0):
        bias = b_in_ref[:, col0:col0 + N_CHUNK]
        return _dot(xn_scr[...], w_in_ref[:, col0 - d:col0 - d + N_CHUNK]) + (bias if scale == 1.0 else scale * bias)

    for c in range(0, d, N_CHUNK):
        cs = slice(c, c + N_CHUNK)
        p_scr[HALO:HALO + t, cs] = _dot(xn_scr[...], w_fold_ref[:, cs]) + b_fold_ref[:, cs]

    row = lax.broadcasted_iota(jnp.int32, (HALO, gw), 0)
    pos1 = seq_idx * t + row + 1
    for gi, w in enumerate(POOL_WINDOWS):
        cols = slice(gi * gw, (gi + 1) * gw)
        src, k, lo, slot = p_scr.at[:, cols], 1, 8, 0
        while 2 * k < w:
            dst = s_scr.at[slot]
            dst[lo:HALO + t, :] = src[lo:HALO + t, :] + src[lo - k:HALO + t - k, :]
            src, k, lo, slot = dst, 2 * k, lo + 8, 1 - slot
        win = src[HALO:HALO + t, :] + src[HALO - k:HALO + t - k, :]
        cur = p_scr[HALO:HALO + t, cols]
        m_scr[:, cols] = win * (1.0 / w) - cur
        cnt = jnp.minimum(pos1, w).astype(_F32)
        m_scr[0:HALO, cols] = win[0:HALO] / cnt - cur[0:HALO]

    for c in range(0, d, N_CHUNK):
        gv_scr[:, c:c + N_CHUNK] = _gelu(proj(2 * d + c))
    for r in range(0, t, ROW_CHUNK):
        gv = gv_scr[r:r + ROW_CHUNK, :]
        mu = jnp.mean(gv, axis=-1, keepdims=True)
        xc = gv - mu
        var = jnp.mean(xc * xc, axis=-1, keepdims=True)
        vn = xc * lax.rsqrt(var + EPS) * ln_g_ref[...] + ln_b_ref[...]
        vn_scr[r:r + ROW_CHUNK, :] = vn.astype(_BF16)
    for c in range(0, d, N_CHUNK):
        u_scr[:, c:c + N_CHUNK] = _gelu(proj(d + c), scale=0.5)

    qi = lax.broadcasted_iota(jnp.int32, (SGU_BLOCK, SGU_BLOCK), 0) // CHUNK
    kj = lax.broadcasted_iota(jnp.int32, (SGU_BLOCK, SGU_BLOCK), 1) // CHUNK
    causal = qi >= kj
    for h in range(n_groups):
        cols = slice(h * gw, (h + 1) * gw)
        ws = jnp.where(causal, ws_ref[h], 0.0).astype(_BF16)
        for r in range(0, t, SGU_BLOCK):
            rows = slice(r, r + SGU_BLOCK)
            sv = _dot(ws, vn_scr[rows, cols]) + bsp_ref[h]
            gated_scr[rows, cols] = (u_scr[rows, cols] * sv).astype(_BF16)
    for h in range(n_groups):
        cols = slice(h * gw, (h + 1) * gw)
        bb_scr[:, cols] = _dot(gated_scr[:, cols], w_sgu_ref[cols, :])

    for c in range(0, d, N_CHUNK):
        cs = slice(c, c + N_CHUNK)
        ta = jnp.tanh(proj(3 * d + c, scale=0.5))
        tb = jnp.tanh(proj(4 * d + c, scale=0.5))
        ha = m_scr[:, cs]
        hb = bb_scr[:, cs]
        merged_scr[:, cs] = ((ha + hb) + (ha * ta + hb * tb)).astype(_BF16)

    _project_norm_residual(merged_scr, w_out_ref, y_scr, x_tile, g_post_ref, o_ref.at[0])


def _ffn_kernel(h_ref, h_prev_ref, g_pre_col_ref, g_post_ref, w1_hbm, w2_hbm, o_ref,
                w1_ref, w2_ref, stage, sem, hn_scr, f_scr, y_scr, *, n_tiles):
    i = pl.program_id(0)
    d_ff = w1_ref.shape[1]

    @pl.when(i == 0)
    def _():
        jobs = _cast_jobs(w1_hbm, w1_ref, row_gain_ref=g_pre_col_ref) + _cast_jobs(w2_hbm, w2_ref)
        _run_weight_jobs(jobs, stage, sem)

    def second_half_prev():
        _project_norm_residual(f_scr, w2_ref, y_scr, h_prev_ref, g_post_ref, o_ref)

    def first_half_cur():
        _pre_norm(h_ref, hn_scr)
        for c in range(0, d_ff, N_CHUNK):
            f = jnp.maximum(_dot(hn_scr[...], w1_ref[:, c:c + N_CHUNK]), 0.0)
            f_scr[:, c:c + N_CHUNK] = (f * f).astype(_BF16)

    @pl.when(i == 0)
    def _():
        first_half_cur()

    @pl.when(jnp.logical_and(i > 0, i < n_tiles))
    def _():
        second_half_prev()
        first_half_cur()

    @pl.when(i == n_tiles)
    def _():
        second_half_prev()


def _resident(shape):
    zeros = (0,) * len(shape)
    return pl.BlockSpec(shape, lambda *_: zeros, pipeline_mode=pl.Buffered(1))


_IN_HBM = pl.BlockSpec(memory_space=pltpu.HBM)


def kernel(x, norm1_pre_g, w_in, b_in, w_pool, pool_scale, sgu_ln_g, sgu_ln_b, w_spatial, b_spatial,
           w_sgu_proj, w_out, norm1_post_g, norm2_pre_g, w_ff1, w_ff2, norm2_post_g):
    bsz, s_len, d = x.shape
    d_in = w_in.shape[1]
    d_ff = w_ff1.shape[1]
    n_heads, blk, _ = w_spatial.shape
    gw = d // n_heads
    t = SEQ_TILE
    assert s_len % t == 0 and t % SGU_BLOCK == 0 and blk == SGU_BLOCK
    assert d % N_CHUNK == 0 and d_ff % N_CHUNK == 0 and d_in == 5 * d
    assert d % W_STAGE_ROWS == 0 and d_ff % W_STAGE_ROWS == 0
    assert d == W_STAGE_COLS and d_ff % W_STAGE_COLS == 0 and gw <= W_STAGE_COLS
    assert len(POOL_WINDOWS) == n_heads == w_pool.shape[0] and max(POOL_WINDOWS) <= HALO

    row = lambda v: v.reshape(1, -1).astype(_F32)
    col = lambda v: v.reshape(-1, 1).astype(_F32)
    w_in, w_out, w_ff1, w_ff2 = w_in.astype(_F32), w_out.astype(_F32), w_ff1.astype(_F32), w_ff2.astype(_F32)
    w_sgu_rows = w_sgu_proj.astype(_F32).reshape(n_heads * gw, gw)
    stage_shape = (W_STAGE_SLOTS, W_STAGE_ROWS, W_STAGE_COLS)

    bsp = jnp.broadcast_to(b_spatial.astype(_F32)[:, :, None], (n_heads, blk, gw))

    mix_vmem_in = (col(norm1_pre_g), row(b_in), w_pool.astype(_F32), row(pool_scale), row(sgu_ln_g),
                   row(sgu_ln_b), w_spatial.astype(_F32), bsp, row(norm1_post_g))
    mix_hbm_in = (w_in, w_sgu_rows, w_out)
    tile_spec = pl.BlockSpec((1, t, d), lambda b, s: (b, s, 0))
    h = pl.pallas_call(
        _mix_kernel,
        grid=(bsz, s_len // t),
        in_specs=[tile_spec] + [_resident(a.shape) for a in mix_vmem_in] + [_IN_HBM] * len(mix_hbm_in),
        out_specs=tile_spec,
        out_shape=jax.ShapeDtypeStruct(x.shape, x.dtype),
        scratch_shapes=[
            pltpu.VMEM((d, d), _BF16),
            pltpu.VMEM((1, d), _F32),
            pltpu.VMEM((d, d_in - d), _BF16),
            pltpu.VMEM(w_sgu_rows.shape, _BF16),
            pltpu.VMEM((d, d), _BF16),
            pltpu.VMEM(stage_shape, _F32),
            pltpu.SemaphoreType.DMA((W_STAGE_SLOTS,)),
            pltpu.VMEM((t, d), _BF16),
            pltpu.VMEM((t + HALO, d), _F32),
            pltpu.VMEM((2, t + HALO, gw), _F32),
            pltpu.VMEM((t, d), _F32),
            pltpu.VMEM((t, d), _BF16),
            pltpu.VMEM((t, d), _F32),
            pltpu.VMEM((t, d), _BF16),
            pltpu.VMEM((t, d), _F32),
            pltpu.VMEM((t, d), _F32),
            pltpu.VMEM((t, d), _BF16),
            pltpu.VMEM((t, d), _F32),
        ],
        compiler_params=pltpu.CompilerParams(
            dimension_semantics=("arbitrary", "arbitrary"),
            vmem_limit_bytes=V7X_VMEM_LIMIT_BYTES),
        name="token_mix",
    )(x, *mix_vmem_in, *mix_hbm_in)

    m = bsz * s_len
    n_tiles = m // t
    cur_spec = pl.BlockSpec((t, d), lambda i: (jnp.minimum(i, n_tiles - 1), 0))
    prev_spec = pl.BlockSpec((t, d), lambda i: (jnp.maximum(i - 1, 0), 0))
    h2 = h.reshape(m, d)
    ffn_vmem_in = (col(norm2_pre_g), row(norm2_post_g))
    out = pl.pallas_call(
        functools.partial(_ffn_kernel, n_tiles=n_tiles),
        grid=(n_tiles + 1,),
        in_specs=[cur_spec, prev_spec] + [_resident(a.shape) for a in ffn_vmem_in] + [_IN_HBM] * 2,
        out_specs=prev_spec,
        out_shape=jax.ShapeDtypeStruct((m, d), x.dtype),
        scratch_shapes=[
            pltpu.VMEM((d, d_ff), _BF16),
            pltpu.VMEM((d_ff, d), _BF16),
            pltpu.VMEM(stage_shape, _F32),
            pltpu.SemaphoreType.DMA((W_STAGE_SLOTS,)),
            pltpu.VMEM((t, d), _BF16),
            pltpu.VMEM((t, d_ff), _BF16),
            pltpu.VMEM((t, d), _F32),
        ],
        compiler_params=pltpu.CompilerParams(
            dimension_semantics=("arbitrary",),
            vmem_limit_bytes=V7X_VMEM_LIMIT_BYTES),
        name="channel_mix",
    )(h2, h2, *ffn_vmem_in, w_ff1, w_ff2)
    return out.reshape(bsz, s_len, d)
```

```python
import math

import jax
import jax.numpy as jnp
from jax import lax
from jax.experimental import pallas as pl
from jax.experimental.pallas import tpu as pltpu

EPS = 1e-6
CHUNK = 64
POOL_WINDOWS = (2, 4, 8, 16)
SGU_BLOCK = 128
HALO = 32

SEQ_TILE = 512
FFN_SUB_TILES = 2
N_CHUNK = 512
ROW_CHUNK = 32
W_STAGE_ROWS = 512
W_STAGE_COLS = 1024
W_STAGE_SLOTS = 4
V7X_VMEM_LIMIT_BYTES = 56 * 1024 * 1024

_GELU_C = math.sqrt(2.0 / math.pi)
_BF16 = jnp.bfloat16
_F32 = jnp.float32


def _gelu(x, scale=1.0):
    inner = x * (_GELU_C + (0.044715 * _GELU_C) * (x * x))
    hx = (0.5 * scale) * x
    return hx + hx * jnp.tanh(inner)


def _dot(a, b):
    return jnp.dot(a, b, preferred_element_type=_F32)


def _dot_exact(a, b):
    a_hi, b_hi = a.astype(_BF16), b.astype(_BF16)
    a_lo = (a - a_hi.astype(_F32)).astype(_BF16)
    b_lo = (b - b_hi.astype(_F32)).astype(_BF16)
    return _dot(a_hi, b_hi) + (_dot(a_hi, b_lo) + _dot(a_lo, b_hi))


def _pre_norm(src_ref, dst_ref):
    for r in range(0, src_ref.shape[0], ROW_CHUNK):
        rows = slice(r, r + ROW_CHUNK)
        x = src_ref[rows, :]
        ms = jnp.mean(x * x, axis=-1, keepdims=True)
        dst_ref[rows, :] = (x * lax.rsqrt(ms + EPS)).astype(_BF16)


def _project_norm_residual(lhs_ref, w_ref, y_scr, res_ref, g_ref, o_ref):
    t, d = y_scr.shape
    for c in range(0, d, N_CHUNK):
        y_scr[:, c:c + N_CHUNK] = _dot(lhs_ref[...], w_ref[:, c:c + N_CHUNK])
    for r in range(0, t, ROW_CHUNK):
        rows = slice(r, r + ROW_CHUNK)
        y = y_scr[rows, :]
        ms = jnp.mean(y * y, axis=-1, keepdims=True)
        o_ref[rows, :] = res_ref[rows, :] + y * lax.rsqrt(ms + EPS) * g_ref[...]


def _column(row):
    return jnp.transpose(jnp.broadcast_to(row, (128, row.shape[1])))[:, 0:1]


def _cast_jobs(src_hbm, dst_vmem, row_gain_ref=None, col_block_scale=None):
    n_rows, n_cols = dst_vmem.shape
    br, bc = min(n_rows, W_STAGE_ROWS), min(n_cols, W_STAGE_COLS)
    jobs = []
    for r in range(0, n_rows, br):
        for c in range(0, n_cols, bc):
            def finish(stage_view, r=r, c=c):
                w = stage_view[...]
                if row_gain_ref is not None:
                    w = w * _column(row_gain_ref[:, r:r + br])
                if col_block_scale is not None and col_block_scale(c) != 1.0:
                    w = w * col_block_scale(c)
                dst_vmem[r:r + br, c:c + bc] = w.astype(_BF16)
            jobs.append((src_hbm.at[pl.ds(r, br), pl.ds(c, bc)], (br, bc), finish))
    return jobs


def _run_weight_jobs(jobs, stage, sem):
    def fetch(k):
        src, (br, bc), _ = jobs[k]
        slot = k % W_STAGE_SLOTS
        return pltpu.make_async_copy(src, stage.at[slot, pl.ds(0, br), pl.ds(0, bc)], sem.at[slot])

    ahead = W_STAGE_SLOTS - 1
    for k in range(min(ahead, len(jobs))):
        fetch(k).start()
    for k, (_, (br, bc), finish) in enumerate(jobs):
        if k + ahead < len(jobs):
            fetch(k + ahead).start()
        fetch(k).wait()
        finish(stage.at[k % W_STAGE_SLOTS, pl.ds(0, br), pl.ds(0, bc)])


def _mix_kernel(x_ref, g_pre_ref, b_in_ref, w_pool_ref, pscale_ref, ln_g_ref, ln_b_ref, ws_ref, b_sp_ref,
                g_post_ref, w_in_hbm, w_sgu_hbm, w_out_hbm, o_ref,
                w_fold_ref, b_fold_ref, w_in_ref, w_sgu_ref, w_out_ref, bsp_ref, stage, sem,
                xn_scr, p_scr, s_scr, gv_scr, vn_scr, u_scr, gated_scr, m_scr, bb_scr, merged_scr, y_scr):
    t, d = x_ref.shape[1], x_ref.shape[2]
    n_groups = len(POOL_WINDOWS)
    gw = d // n_groups
    seq_idx = pl.program_id(1)
    x_tile = x_ref.at[0]

    @pl.when(jnp.logical_and(pl.program_id(0) == 0, seq_idx == 0))
    def _():
        def fold_rows(stage_view, r):
            gain = _column(g_pre_ref[:, r:r + W_STAGE_ROWS])
            for gi in range(n_groups):
                cols = slice(gi * gw, (gi + 1) * gw)
                wf = _dot_exact(stage_view[:, cols] * gain, w_pool_ref[gi]) * (0.5 * pscale_ref[:, cols])
                w_fold_ref[r:r + W_STAGE_ROWS, cols] = wf.astype(_BF16)

        jobs = [(w_in_hbm.at[pl.ds(r, W_STAGE_ROWS), pl.ds(0, d)], (W_STAGE_ROWS, d),
                 lambda view, r=r: fold_rows(view, r)) for r in range(0, d, W_STAGE_ROWS)]
        jobs += _cast_jobs(w_in_hbm.at[:, pl.ds(d, w_in_ref.shape[1])], w_in_ref, row_gain_ref=g_pre_ref,
                           col_block_scale=lambda c: 0.5 if c >= 2 * d else 1.0)
        jobs += _cast_jobs(w_sgu_hbm, w_sgu_ref) + _cast_jobs(w_out_hbm, w_out_ref)
        _run_weight_jobs(jobs, stage, sem)
        for gi in range(n_groups):
            cols = slice(gi * gw, (gi + 1) * gw)
            b8 = jnp.broadcast_to(b_in_ref[:, cols], (8, gw))
            b_fold_ref[:, cols] = _dot_exact(b8, w_pool_ref[gi])[0:1] * (0.5 * pscale_ref[:, cols])
        for h in range(n_groups):
            bsp_ref[h] = jnp.broadcast_to(_column(b_sp_ref[h:h + 1, :]), bsp_ref.shape[1:])

    @pl.when(seq_idx == 0)
    def _():
        p_scr[0:HALO, :] = jnp.zeros((HALO, d), _F32)

    @pl.when(seq_idx > 0)
    def _():
        p_scr[0:HALO, :] = p_scr[t:t + HALO, :]

    _pre_norm(x_tile, xn_scr)

    def proj(col0, scale=1.0):
        bias = b_in_ref[:, col0:col0 + N_CHUNK]
        return _dot(xn_scr[...], w_in_ref[:, col0 - d:col0 - d + N_CHUNK]) + (bias if scale == 1.0 else scale * bias)

    for c in range(0, d, N_CHUNK):
        cs = slice(c, c + N_CHUNK)
        p_scr[HALO:HALO + t, cs] = _dot(xn_scr[...], w_fold_ref[:, cs]) + b_fold_ref[:, cs]

    row = lax.broadcasted_iota(jnp.int32, (HALO, gw), 0)
    pos1 = seq_idx * t + row + 1
    for gi, w in enumerate(POOL_WINDOWS):
        cols = slice(gi * gw, (gi + 1) * gw)
        src, k, lo, slot = p_scr.at[:, cols], 1, 8, 0
        while 2 * k < w:
            dst = s_scr.at[slot]
            dst[lo:HALO + t, :] = src[lo:HALO + t, :] + src[lo - k:HALO + t - k, :]
            src, k, lo, slot = dst, 2 * k, lo + 8, 1 - slot
        win = src[HALO:HALO + t, :] + src[HALO - k:HALO + t - k, :]
        cur = p_scr[HALO:HALO + t, cols]
        m_scr[:, cols] = win * (1.0 / w) - cur
        cnt = jnp.minimum(pos1, w).astype(_F32)
        m_scr[0:HALO, cols] = win[0:HALO] / cnt - cur[0:HALO]

    for c in range(0, d, N_CHUNK):
        gv_scr[:, c:c + N_CHUNK] = _gelu(proj(2 * d + c))
    for r in range(0, t, ROW_CHUNK):
        gv = gv_scr[r:r + ROW_CHUNK, :]
        mu = jnp.mean(gv, axis=-1, keepdims=True)
        xc = gv - mu
        var = jnp.mean(xc * xc, axis=-1, keepdims=True)
        vn = xc * lax.rsqrt(var + EPS) * ln_g_ref[...] + ln_b_ref[...]
        vn_scr[r:r + ROW_CHUNK, :] = vn.astype(_BF16)
    for c in range(0, d, N_CHUNK):
        u_scr[:, c:c + N_CHUNK] = _gelu(proj(d + c), scale=0.5)

    qi = lax.broadcasted_iota(jnp.int32, (SGU_BLOCK, SGU_BLOCK), 0) // CHUNK
    kj = lax.broadcasted_iota(jnp.int32, (SGU_BLOCK, SGU_BLOCK), 1) // CHUNK
    causal = qi >= kj
    for h in range(n_groups):
        cols = slice(h * gw, (h + 1) * gw)
        ws = jnp.where(causal, ws_ref[h], 0.0).astype(_BF16)
        for r in range(0, t, SGU_BLOCK):
            rows = slice(r, r + SGU_BLOCK)
            sv = _dot(ws, vn_scr[rows, cols]) + bsp_ref[h]
            gated_scr[rows, cols] = (u_scr[rows, cols] * sv).astype(_BF16)
    for h in range(n_groups):
        cols = slice(h * gw, (h + 1) * gw)
        bb_scr[:, cols] = _dot(gated_scr[:, cols], w_sgu_ref[cols, :])

    for c in range(0, d, N_CHUNK):
        cs = slice(c, c + N_CHUNK)
        ta = jnp.tanh(proj(3 * d + c, scale=0.5))
        tb = jnp.tanh(proj(4 * d + c, scale=0.5))
        ha = m_scr[:, cs]
        hb = bb_scr[:, cs]
        merged_scr[:, cs] = ((ha + hb) + (ha * ta + hb * tb)).astype(_BF16)

    _project_norm_residual(merged_scr, w_out_ref, y_scr, x_tile, g_post_ref, o_ref.at[0])


def _ffn_kernel(h_ref, g_pre_ref, g_post_ref, w1_hbm, w2_hbm, o_ref,
                w1_ref, w2_ref, stage, sem, hn_scr, f_scr, y_scr):
    d_ff = w1_ref.shape[1]

    @pl.when(pl.program_id(0) == 0)
    def _():
        jobs = _cast_jobs(w1_hbm, w1_ref, row_gain_ref=g_pre_ref) + _cast_jobs(w2_hbm, w2_ref)
        _run_weight_jobs(jobs, stage, sem)

    for r0 in range(0, h_ref.shape[0], SEQ_TILE):
        rs = pl.ds(r0, SEQ_TILE)
        hn, f_sub = hn_scr.at[rs, :], f_scr.at[rs, :]
        _pre_norm(h_ref.at[rs, :], hn)
        for c in range(0, d_ff, N_CHUNK):
            f = jnp.maximum(_dot(hn[...], w1_ref[:, c:c + N_CHUNK]), 0.0)
            f_sub[:, c:c + N_CHUNK] = (f * f).astype(_BF16)
        _project_norm_residual(f_sub, w2_ref, y_scr.at[rs, :], h_ref.at[rs, :], g_post_ref, o_ref.at[rs, :])


def _resident(shape):
    zeros = (0,) * len(shape)
    return pl.BlockSpec(shape, lambda *_: zeros, pipeline_mode=pl.Buffered(1))


_IN_HBM = pl.BlockSpec(memory_space=pltpu.HBM)


def kernel(x, norm1_pre_g, w_in, b_in, w_pool, pool_scale, sgu_ln_g, sgu_ln_b, w_spatial, b_spatial,
           w_sgu_proj, w_out, norm1_post_g, norm2_pre_g, w_ff1, w_ff2, norm2_post_g):
    bsz, s_len, d = x.shape
    d_in = w_in.shape[1]
    d_ff = w_ff1.shape[1]
    n_heads, blk, _ = w_spatial.shape
    gw = d // n_heads
    t = SEQ_TILE
    assert s_len % t == 0 and t % SGU_BLOCK == 0 and blk == SGU_BLOCK
    assert d % N_CHUNK == 0 and d_ff % N_CHUNK == 0 and d_in == 5 * d
    assert d % W_STAGE_ROWS == 0 and d_ff % W_STAGE_ROWS == 0
    assert d == W_STAGE_COLS and d_ff % W_STAGE_COLS == 0 and gw <= W_STAGE_COLS
    assert len(POOL_WINDOWS) == n_heads == w_pool.shape[0] and max(POOL_WINDOWS) <= HALO

    row = lambda v: v.reshape(1, -1).astype(_F32)
    w_in, w_out, w_ff1, w_ff2 = w_in.astype(_F32), w_out.astype(_F32), w_ff1.astype(_F32), w_ff2.astype(_F32)
    w_sgu_rows = w_sgu_proj.astype(_F32).reshape(n_heads * gw, gw)
    stage_shape = (W_STAGE_SLOTS, W_STAGE_ROWS, W_STAGE_COLS)

    mix_vmem_in = (row(norm1_pre_g), row(b_in), w_pool.astype(_F32), row(pool_scale), row(sgu_ln_g),
                   row(sgu_ln_b), w_spatial.astype(_F32), b_spatial.astype(_F32), row(norm1_post_g))
    mix_hbm_in = (w_in, w_sgu_rows, w_out)
    tile_spec = pl.BlockSpec((1, t, d), lambda b, s: (b, s, 0))
    h = pl.pallas_call(
        _mix_kernel,
        grid=(bsz, s_len // t),
        in_specs=[tile_spec] + [_resident(a.shape) for a in mix_vmem_in] + [_IN_HBM] * len(mix_hbm_in),
        out_specs=tile_spec,
        out_shape=jax.ShapeDtypeStruct(x.shape, x.dtype),
        scratch_shapes=[
            pltpu.VMEM((d, d), _BF16),
            pltpu.VMEM((1, d), _F32),
            pltpu.VMEM((d, d_in - d), _BF16),
            pltpu.VMEM(w_sgu_rows.shape, _BF16),
            pltpu.VMEM((d, d), _BF16),
            pltpu.VMEM((n_heads, blk, gw), _F32),
            pltpu.VMEM(stage_shape, _F32),
            pltpu.SemaphoreType.DMA((W_STAGE_SLOTS,)),
            pltpu.VMEM((t, d), _BF16),
            pltpu.VMEM((t + HALO, d), _F32),
            pltpu.VMEM((2, t + HALO, gw), _F32),
            pltpu.VMEM((t, d), _F32),
            pltpu.VMEM((t, d), _BF16),
            pltpu.VMEM((t, d), _F32),
            pltpu.VMEM((t, d), _BF16),
            pltpu.VMEM((t, d), _F32),
            pltpu.VMEM((t, d), _F32),
            pltpu.VMEM((t, d), _BF16),
            pltpu.VMEM((t, d), _F32),
        ],
        compiler_params=pltpu.CompilerParams(
            dimension_semantics=("arbitrary", "arbitrary"),
            vmem_limit_bytes=V7X_VMEM_LIMIT_BYTES),
        name="token_mix",
    )(x, *mix_vmem_in, *mix_hbm_in)

    m = bsz * s_len
    t_ffn = FFN_SUB_TILES * t
    assert m % t_ffn == 0
    row_spec = pl.BlockSpec((t_ffn, d), lambda i: (i, 0))
    ffn_vmem_in = (row(norm2_pre_g), row(norm2_post_g))
    out = pl.pallas_call(
        _ffn_kernel,
        grid=(m // t_ffn,),
        in_specs=[row_spec] + [_resident(a.shape) for a in ffn_vmem_in] + [_IN_HBM] * 2,
        out_specs=row_spec,
        out_shape=jax.ShapeDtypeStruct((m, d), x.dtype),
        scratch_shapes=[
            pltpu.VMEM((d, d_ff), _BF16),
            pltpu.VMEM((d_ff, d), _BF16),
            pltpu.VMEM(stage_shape, _F32),
            pltpu.SemaphoreType.DMA((W_STAGE_SLOTS,)),
            pltpu.VMEM((t_ffn, d), _BF16),
            pltpu.VMEM((t_ffn, d_ff), _BF16),
            pltpu.VMEM((t_ffn, d), _F32),
        ],
        compiler_params=pltpu.CompilerParams(
            dimension_semantics=("arbitrary",),
            vmem_limit_bytes=V7X_VMEM_LIMIT_BYTES),
        name="channel_mix",
    )(h.reshape(m, d), *ffn_vmem_in, w_ff1, w_ff2)
    return out.reshape(bsz, s_len, d)
```

```python
import math

import jax
import jax.numpy as jnp
from jax import lax
from jax.experimental import pallas as pl
from jax.experimental.pallas import tpu as pltpu

EPS = 1e-6
CHUNK = 64
POOL_WINDOWS = (2, 4, 8, 16)
SGU_BLOCK = 128
HALO = 32

SEQ_TILE = 512
FFN_SUB_TILES = 2
N_CHUNK = 512
ROW_CHUNK = 32
W_STAGE_ROWS = 512
W_STAGE_COLS = 1024
W_STAGE_SLOTS = 4
V7X_VMEM_LIMIT_BYTES = 56 * 1024 * 1024

_GELU_C = math.sqrt(2.0 / math.pi)
_BF16 = jnp.bfloat16
_F32 = jnp.float32


def _gelu(x, scale=1.0):
    inner = x * (_GELU_C + (0.044715 * _GELU_C) * (x * x))
    hx = (0.5 * scale) * x
    return hx + hx * jnp.tanh(inner)


def _dot(a, b):
    return jnp.dot(a, b, preferred_element_type=_F32)


def _dot_exact(a, b):
    a_hi, b_hi = a.astype(_BF16), b.astype(_BF16)
    a_lo = (a - a_hi.astype(_F32)).astype(_BF16)
    b_lo = (b - b_hi.astype(_F32)).astype(_BF16)
    return _dot(a_hi, b_hi) + (_dot(a_hi, b_lo) + _dot(a_lo, b_hi))


def _pre_norm(src_ref, dst_ref):
    for r in range(0, src_ref.shape[0], ROW_CHUNK):
        rows = slice(r, r + ROW_CHUNK)
        x = src_ref[rows, :]
        ms = jnp.mean(x * x, axis=-1, keepdims=True)
        dst_ref[rows, :] = (x * lax.rsqrt(ms + EPS)).astype(_BF16)


def _project_norm_residual(lhs_ref, w_ref, y_scr, res_ref, g_ref, o_ref):
    t, d = y_scr.shape
    for c in range(0, d, N_CHUNK):
        y_scr[:, c:c + N_CHUNK] = _dot(lhs_ref[...], w_ref[:, c:c + N_CHUNK])
    for r in range(0, t, ROW_CHUNK):
        rows = slice(r, r + ROW_CHUNK)
        y = y_scr[rows, :]
        ms = jnp.mean(y * y, axis=-1, keepdims=True)
        o_ref[rows, :] = res_ref[rows, :] + y * lax.rsqrt(ms + EPS) * g_ref[...]


def _column(row):
    return jnp.transpose(jnp.broadcast_to(row, (128, row.shape[1])))[:, 0:1]


def _cast_jobs(src_hbm, dst_vmem, row_gain_ref=None, col_block_scale=None):
    n_rows, n_cols = dst_vmem.shape
    br, bc = min(n_rows, W_STAGE_ROWS), min(n_cols, W_STAGE_COLS)
    jobs = []
    for r in range(0, n_rows, br):
        for c in range(0, n_cols, bc):
            def finish(stage_view, r=r, c=c):
                w = stage_view[...]
                if row_gain_ref is not None:
                    w = w * _column(row_gain_ref[:, r:r + br])
                if col_block_scale is not None and col_block_scale(c) != 1.0:
                    w = w * col_block_scale(c)
                dst_vmem[r:r + br, c:c + bc] = w.astype(_BF16)
            jobs.append((src_hbm.at[pl.ds(r, br), pl.ds(c, bc)], (br, bc), finish))
    return jobs


def _run_weight_jobs(jobs, stage, sem):
    def fetch(k):
        src, (br, bc), _ = jobs[k]
        slot = k % W_STAGE_SLOTS
        return pltpu.make_async_copy(src, stage.at[slot, pl.ds(0, br), pl.ds(0, bc)], sem.at[slot])

    ahead = W_STAGE_SLOTS - 1
    for k in range(min(ahead, len(jobs))):
        fetch(k).start()
    for k, (_, (br, bc), finish) in enumerate(jobs):
        if k + ahead < len(jobs):
            fetch(k + ahead).start()
        fetch(k).wait()
        finish(stage.at[k % W_STAGE_SLOTS, pl.ds(0, br), pl.ds(0, bc)])


def _mix_kernel(x_ref, g_pre_ref, b_in_ref, w_pool_ref, pscale_ref, ln_g_ref, ln_b_ref, ws_ref, b_sp_ref,
                g_post_ref, w_in_hbm, w_sgu_hbm, w_out_hbm, o_ref,
                w_fold_ref, b_fold_ref, w_in_ref, w_sgu_ref, w_out_ref, bsp_ref, stage, sem,
                xn_scr, p_scr, s_scr, gv_scr, vn_scr, u_scr, gated_scr, m_scr, bb_scr, merged_scr, y_scr):
    t, d = x_ref.shape[1], x_ref.shape[2]
    n_groups = len(POOL_WINDOWS)
    gw = d // n_groups
    seq_idx = pl.program_id(1)
    x_tile = x_ref.at[0]

    @pl.when(jnp.logical_and(pl.program_id(0) == 0, seq_idx == 0))
    def _():
        def fold_rows(stage_view, r):
            gain = _column(g_pre_ref[:, r:r + W_STAGE_ROWS])
            for gi in range(n_groups):
                cols = slice(gi * gw, (gi + 1) * gw)
                wf = _dot_exact(stage_view[:, cols] * gain, w_pool_ref[gi]) * (0.5 * pscale_ref[:, cols])
                w_fold_ref[r:r + W_STAGE_ROWS, cols] = wf.astype(_BF16)

        jobs = [(w_in_hbm.at[pl.ds(r, W_STAGE_ROWS), pl.ds(0, d)], (W_STAGE_ROWS, d),
                 lambda view, r=r: fold_rows(view, r)) for r in range(0, d, W_STAGE_ROWS)]
        jobs += _cast_jobs(w_in_hbm.at[:, pl.ds(d, w_in_ref.shape[1])], w_in_ref, row_gain_ref=g_pre_ref,
                           col_block_scale=lambda c: 0.5 if c >= 2 * d else 1.0)
        jobs += _cast_jobs(w_sgu_hbm, w_sgu_ref) + _cast_jobs(w_out_hbm, w_out_ref)
        _run_weight_jobs(jobs, stage, sem)
        for gi in range(n_groups):
            cols = slice(gi * gw, (gi + 1) * gw)
            b8 = jnp.broadcast_to(b_in_ref[:, cols], (8, gw))
            b_fold_ref[:, cols] = _dot_exact(b8, w_pool_ref[gi])[0:1] * (0.5 * pscale_ref[:, cols])
        for h in range(n_groups):
            bsp_ref[h] = jnp.broadcast_to(_column(b_sp_ref[h:h + 1, :]), bsp_ref.shape[1:])

    @pl.when(seq_idx == 0)
    def _():
        p_scr[0:HALO, :] = jnp.zeros((HALO, d), _F32)

    @pl.when(seq_idx > 0)
    def _():
        p_scr[0:HALO, :] = p_scr[t:t + HALO, :]

    _pre_norm(x_tile, xn_scr)

    def proj(col0, scale=1.0):
        bias = b_in_ref[:, col0:col0 + N_CHUNK]
        return _dot(xn_scr[...], w_in_ref[:, col0 - d:col0 - d + N_CHUNK]) + (bias if scale == 1.0 else scale * bias)

    for c in range(0, d, N_CHUNK):
        cs = slice(c, c + N_CHUNK)
        p_scr[HALO:HALO + t, cs] = _dot(xn_scr[...], w_fold_ref[:, cs]) + b_fold_ref[:, cs]

    row = lax.broadcasted_iota(jnp.int32, (HALO, gw), 0)
    pos1 = seq_idx * t + row + 1
    for gi, w in enumerate(POOL_WINDOWS):
        cols = slice(gi * gw, (gi + 1) * gw)
        src, k, lo, slot = p_scr.at[:, cols], 1, 8, 0
        while 2 * k < w:
            dst = s_scr.at[slot]
            dst[lo:HALO + t, :] = src[lo:HALO + t, :] + src[lo - k:HALO + t - k, :]
            src, k, lo, slot = dst, 2 * k, lo + 8, 1 - slot
        win = src[HALO:HALO + t, :] + src[HALO - k:HALO + t - k, :]
        cur = p_scr[HALO:HALO + t, cols]
        m_scr[:, cols] = win * (1.0 / w) - cur
        cnt = jnp.minimum(pos1, w).astype(_F32)
        m_scr[0:HALO, cols] = win[0:HALO] / cnt - cur[0:HALO]

    for c in range(0, d, N_CHUNK):
        gv_scr[:, c:c + N_CHUNK] = _gelu(proj(2 * d + c))
    for r in range(0, t, ROW_CHUNK):
        gv = gv_scr[r:r + ROW_CHUNK, :]
        mu = jnp.mean(gv, axis=-1, keepdims=True)
        xc = gv - mu
        var = jnp.mean(xc * xc, axis=-1, keepdims=True)
        vn = xc * lax.rsqrt(var + EPS) * ln_g_ref[...] + ln_b_ref[...]
        vn_scr[r:r + ROW_CHUNK, :] = vn.astype(_BF16)
    for c in range(0, d, N_CHUNK):
        u_scr[:, c:c + N_CHUNK] = _gelu(proj(d + c), scale=0.5)

    qi = lax.broadcasted_iota(jnp.int32, (SGU_BLOCK, SGU_BLOCK), 0) // CHUNK
    kj = lax.broadcasted_iota(jnp.int32, (SGU_BLOCK, SGU_BLOCK), 1) // CHUNK
    causal = qi >= kj
    for h in range(n_groups):
        cols = slice(h * gw, (h + 1) * gw)
        ws = jnp.where(causal, ws_ref[h], 0.0).astype(_BF16)
        for r in range(0, t, SGU_BLOCK):
            rows = slice(r, r + SGU_BLOCK)
            sv = _dot(ws, vn_scr[rows, cols]) + bsp_ref[h]
            gated_scr[rows, cols] = (u_scr[rows, cols] * sv).astype(_BF16)
    for h in range(n_groups):
        cols = slice(h * gw, (h + 1) * gw)
        bb_scr[:, cols] = _dot(gated_scr[:, cols], w_sgu_ref[cols, :])

    for c in range(0, d, N_CHUNK):
        cs = slice(c, c + N_CHUNK)
        ta = jnp.tanh(proj(3 * d + c, scale=0.5))
        tb = jnp.tanh(proj(4 * d + c, scale=0.5))
        ha = m_scr[:, cs]
        hb = bb_scr[:, cs]
        merged_scr[:, cs] = ((ha + hb) + (ha * ta + hb * tb)).astype(_BF16)

    _project_norm_residual(merged_scr, w_out_ref, y_scr, x_tile, g_post_ref, o_ref.at[0])


def _ffn_kernel(h_ref, g_pre_ref, g_post_ref, w1_hbm, w2_hbm, o_ref,
                w1_ref, w2_ref, stage, sem, hn_scr, f_scr, y_scr):
    d_ff = w1_ref.shape[1]

    @pl.when(pl.program_id(0) == 0)
    def _():
        jobs = _cast_jobs(w1_hbm, w1_ref, row_gain_ref=g_pre_ref) + _cast_jobs(w2_hbm, w2_ref)
        _run_weight_jobs(jobs, stage, sem)

    groups = [pl.ds(r0, SEQ_TILE) for r0 in range(0, h_ref.shape[0], SEQ_TILE)]
    _pre_norm(h_ref, hn_scr)
    for c in range(0, d_ff, N_CHUNK):
        f = jnp.maximum(_dot(hn_scr[...], w1_ref[:, c:c + N_CHUNK]), 0.0)
        f_scr[:, c:c + N_CHUNK] = (f * f).astype(_BF16)
    for rs in groups:
        _project_norm_residual(f_scr.at[rs, :], w2_ref, y_scr.at[rs, :], h_ref.at[rs, :], g_post_ref,
                               o_ref.at[rs, :])


def _resident(shape):
    zeros = (0,) * len(shape)
    return pl.BlockSpec(shape, lambda *_: zeros, pipeline_mode=pl.Buffered(1))


_IN_HBM = pl.BlockSpec(memory_space=pltpu.HBM)


def kernel(x, norm1_pre_g, w_in, b_in, w_pool, pool_scale, sgu_ln_g, sgu_ln_b, w_spatial, b_spatial,
           w_sgu_proj, w_out, norm1_post_g, norm2_pre_g, w_ff1, w_ff2, norm2_post_g):
    bsz, s_len, d = x.shape
    d_in = w_in.shape[1]
    d_ff = w_ff1.shape[1]
    n_heads, blk, _ = w_spatial.shape
    gw = d // n_heads
    t = SEQ_TILE
    assert s_len % t == 0 and t % SGU_BLOCK == 0 and blk == SGU_BLOCK
    assert d % N_CHUNK == 0 and d_ff % N_CHUNK == 0 and d_in == 5 * d
    assert d % W_STAGE_ROWS == 0 and d_ff % W_STAGE_ROWS == 0
    assert d == W_STAGE_COLS and d_ff % W_STAGE_COLS == 0 and gw <= W_STAGE_COLS
    assert len(POOL_WINDOWS) == n_heads == w_pool.shape[0] and max(POOL_WINDOWS) <= HALO

    row = lambda v: v.reshape(1, -1).astype(_F32)
    w_in, w_out, w_ff1, w_ff2 = w_in.astype(_F32), w_out.astype(_F32), w_ff1.astype(_F32), w_ff2.astype(_F32)
    w_sgu_rows = w_sgu_proj.astype(_F32).reshape(n_heads * gw, gw)
    stage_shape = (W_STAGE_SLOTS, W_STAGE_ROWS, W_STAGE_COLS)

    mix_vmem_in = (row(norm1_pre_g), row(b_in), w_pool.astype(_F32), row(pool_scale), row(sgu_ln_g),
                   row(sgu_ln_b), w_spatial.astype(_F32), b_spatial.astype(_F32), row(norm1_post_g))
    mix_hbm_in = (w_in, w_sgu_rows, w_out)
    tile_spec = pl.BlockSpec((1, t, d), lambda b, s: (b, s, 0))
    h = pl.pallas_call(
        _mix_kernel,
        grid=(bsz, s_len // t),
        in_specs=[tile_spec] + [_resident(a.shape) for a in mix_vmem_in] + [_IN_HBM] * len(mix_hbm_in),
        out_specs=tile_spec,
        out_shape=jax.ShapeDtypeStruct(x.shape, x.dtype),
        scratch_shapes=[
            pltpu.VMEM((d, d), _BF16),
            pltpu.VMEM((1, d), _F32),
            pltpu.VMEM((d, d_in - d), _BF16),
            pltpu.VMEM(w_sgu_rows.shape, _BF16),
            pltpu.VMEM((d, d), _BF16),
            pltpu.VMEM((n_heads, blk, gw), _F32),
            pltpu.VMEM(stage_shape, _F32),
            pltpu.SemaphoreType.DMA((W_STAGE_SLOTS,)),
            pltpu.VMEM((t, d), _BF16),
            pltpu.VMEM((t + HALO, d), _F32),
            pltpu.VMEM((2, t + HALO, gw), _F32),
            pltpu.VMEM((t, d), _F32),
            pltpu.VMEM((t, d), _BF16),
            pltpu.VMEM((t, d), _F32),
            pltpu.VMEM((t, d), _BF16),
            pltpu.VMEM((t, d), _F32),
            pltpu.VMEM((t, d), _F32),
            pltpu.VMEM((t, d), _BF16),
            pltpu.VMEM((t, d), _F32),
        ],
        compiler_params=pltpu.CompilerParams(
            dimension_semantics=("arbitrary", "arbitrary"),
            vmem_limit_bytes=V7X_VMEM_LIMIT_BYTES),
        name="token_mix",
    )(x, *mix_vmem_in, *mix_hbm_in)

    m = bsz * s_len
    t_ffn = FFN_SUB_TILES * t
    assert m % t_ffn == 0
    row_spec = pl.BlockSpec((t_ffn, d), lambda i: (i, 0))
    ffn_vmem_in = (row(norm2_pre_g), row(norm2_post_g))
    out = pl.pallas_call(
        _ffn_kernel,
        grid=(m // t_ffn,),
        in_specs=[row_spec] + [_resident(a.shape) for a in ffn_vmem_in] + [_IN_HBM] * 2,
        out_specs=row_spec,
        out_shape=jax.ShapeDtypeStruct((m, d), x.dtype),
        scratch_shapes=[
            pltpu.VMEM((d, d_ff), _BF16),
            pltpu.VMEM((d_ff, d), _BF16),
            pltpu.VMEM(stage_shape, _F32),
            pltpu.SemaphoreType.DMA((W_STAGE_SLOTS,)),
            pltpu.VMEM((t_ffn, d), _BF16),
            pltpu.VMEM((t_ffn, d_ff), _BF16),
            pltpu.VMEM((t_ffn, d), _F32),
        ],
        compiler_params=pltpu.CompilerParams(
            dimension_semantics=("arbitrary",),
            vmem_limit_bytes=V7X_VMEM_LIMIT_BYTES),
        name="channel_mix",
    )(h.reshape(m, d), *ffn_vmem_in, w_ff1, w_ff2)
    return out.reshape(bsz, s_len, d)
```

```python
import math

import jax
import jax.numpy as jnp
from jax import lax
from jax.experimental import pallas as pl
from jax.experimental.pallas import tpu as pltpu

EPS = 1e-6
CHUNK = 64
POOL_WINDOWS = (2, 4, 8, 16)
SGU_BLOCK = 128
HALO = 32

SEQ_TILE = 512
FFN_SUB_TILES = 2
N_CHUNK = 512
ROW_CHUNK = 32
W_STAGE_ROWS = 512
W_STAGE_COLS = 1024
W_STAGE_SLOTS = 4
V7X_VMEM_LIMIT_BYTES = 56 * 1024 * 1024

_GELU_C = math.sqrt(2.0 / math.pi)
_BF16 = jnp.bfloat16
_F32 = jnp.float32


def _gelu(x, scale=1.0):
    inner = x * (_GELU_C + (0.044715 * _GELU_C) * (x * x))
    hx = (0.5 * scale) * x
    return hx + hx * jnp.tanh(inner)


def _dot(a, b):
    return jnp.dot(a, b, preferred_element_type=_F32)


def _dot_exact(a, b):
    a_hi, b_hi = a.astype(_BF16), b.astype(_BF16)
    a_lo = (a - a_hi.astype(_F32)).astype(_BF16)
    b_lo = (b - b_hi.astype(_F32)).astype(_BF16)
    return _dot(a_hi, b_hi) + (_dot(a_hi, b_lo) + _dot(a_lo, b_hi))


def _pre_norm(src_ref, dst_ref):
    for r in range(0, src_ref.shape[0], ROW_CHUNK):
        rows = slice(r, r + ROW_CHUNK)
        x = src_ref[rows, :]
        ms = jnp.mean(x * x, axis=-1, keepdims=True)
        dst_ref[rows, :] = (x * lax.rsqrt(ms + EPS)).astype(_BF16)


def _project_norm_residual(lhs_ref, w_ref, y_scr, res_ref, g_ref, o_ref):
    t, d = y_scr.shape
    last = d - N_CHUNK
    for c in range(0, last, N_CHUNK):
        y_scr[:, c:c + N_CHUNK] = _dot(lhs_ref[...], w_ref[:, c:c + N_CHUNK])
    half = t // 2
    for r0 in range(0, t, half):
        y_scr[r0:r0 + half, last:] = _dot(lhs_ref[r0:r0 + half, :], w_ref[:, last:])
        for r in range(r0, r0 + half, ROW_CHUNK):
            rows = slice(r, r + ROW_CHUNK)
            y = y_scr[rows, :]
            ms = jnp.mean(y * y, axis=-1, keepdims=True)
            o_ref[rows, :] = res_ref[rows, :] + y * lax.rsqrt(ms + EPS) * g_ref[...]


def _column(row):
    return jnp.transpose(jnp.broadcast_to(row, (128, row.shape[1])))[:, 0:1]


def _cast_jobs(src_hbm, dst_vmem, row_gain_ref=None, col_block_scale=None):
    n_rows, n_cols = dst_vmem.shape
    br, bc = min(n_rows, W_STAGE_ROWS), min(n_cols, W_STAGE_COLS)
    jobs = []
    for r in range(0, n_rows, br):
        for c in range(0, n_cols, bc):
            def finish(stage_view, r=r, c=c):
                w = stage_view[...]
                if row_gain_ref is not None:
                    w = w * _column(row_gain_ref[:, r:r + br])
                if col_block_scale is not None and col_block_scale(c) != 1.0:
                    w = w * col_block_scale(c)
                dst_vmem[r:r + br, c:c + bc] = w.astype(_BF16)
            jobs.append((src_hbm.at[pl.ds(r, br), pl.ds(c, bc)], (br, bc), finish))
    return jobs


def _run_weight_jobs(jobs, stage, sem):
    def fetch(k):
        src, (br, bc), _ = jobs[k]
        slot = k % W_STAGE_SLOTS
        return pltpu.make_async_copy(src, stage.at[slot, pl.ds(0, br), pl.ds(0, bc)], sem.at[slot])

    ahead = W_STAGE_SLOTS - 1
    for k in range(min(ahead, len(jobs))):
        fetch(k).start()
    for k, (_, (br, bc), finish) in enumerate(jobs):
        if k + ahead < len(jobs):
            fetch(k + ahead).start()
        fetch(k).wait()
        finish(stage.at[k % W_STAGE_SLOTS, pl.ds(0, br), pl.ds(0, bc)])


def _mix_kernel(x_ref, g_pre_ref, b_in_ref, w_pool_ref, pscale_ref, ln_g_ref, ln_b_ref, ws_ref, b_sp_ref,
                g_post_ref, w_in_hbm, w_sgu_hbm, w_out_hbm, o_ref,
                w_fold_ref, b_fold_ref, w_in_ref, w_sgu_ref, w_out_ref, bsp_ref, stage, sem,
                xn_scr, p_scr, s_scr, gv_scr, vn_scr, u_scr, gated_scr, m_scr, bb_scr, merged_scr, y_scr):
    t, d = x_ref.shape[1], x_ref.shape[2]
    n_groups = len(POOL_WINDOWS)
    gw = d // n_groups
    seq_idx = pl.program_id(1)
    x_tile = x_ref.at[0]

    @pl.when(jnp.logical_and(pl.program_id(0) == 0, seq_idx == 0))
    def _():
        def fold_rows(stage_view, r):
            gain = _column(g_pre_ref[:, r:r + W_STAGE_ROWS])
            for gi in range(n_groups):
                cols = slice(gi * gw, (gi + 1) * gw)
                wf = _dot_exact(stage_view[:, cols] * gain, w_pool_ref[gi]) * (0.5 * pscale_ref[:, cols])
                w_fold_ref[r:r + W_STAGE_ROWS, cols] = wf.astype(_BF16)

        jobs = [(w_in_hbm.at[pl.ds(r, W_STAGE_ROWS), pl.ds(0, d)], (W_STAGE_ROWS, d),
                 lambda view, r=r: fold_rows(view, r)) for r in range(0, d, W_STAGE_ROWS)]
        jobs += _cast_jobs(w_in_hbm.at[:, pl.ds(d, w_in_ref.shape[1])], w_in_ref, row_gain_ref=g_pre_ref,
                           col_block_scale=lambda c: 0.5 if c >= 2 * d else 1.0)
        jobs += _cast_jobs(w_sgu_hbm, w_sgu_ref) + _cast_jobs(w_out_hbm, w_out_ref)
        _run_weight_jobs(jobs, stage, sem)
        for gi in range(n_groups):
            cols = slice(gi * gw, (gi + 1) * gw)
            b8 = jnp.broadcast_to(b_in_ref[:, cols], (8, gw))
            b_fold_ref[:, cols] = _dot_exact(b8, w_pool_ref[gi])[0:1] * (0.5 * pscale_ref[:, cols])
        for h in range(n_groups):
            bsp_ref[h] = jnp.broadcast_to(_column(b_sp_ref[h:h + 1, :]), bsp_ref.shape[1:])

    @pl.when(seq_idx == 0)
    def _():
        p_scr[0:HALO, :] = jnp.zeros((HALO, d), _F32)

    @pl.when(seq_idx > 0)
    def _():
        p_scr[0:HALO, :] = p_scr[t:t + HALO, :]

    _pre_norm(x_tile, xn_scr)

    def proj(col0, scale=1.0):
        bias = b_in_ref[:, col0:col0 + N_CHUNK]
        return _dot(xn_scr[...], w_in_ref[:, col0 - d:col0 - d + N_CHUNK]) + (bias if scale == 1.0 else scale * bias)

    for c in range(0, d, N_CHUNK):
        cs = slice(c, c + N_CHUNK)
        p_scr[HALO:HALO + t, cs] = _dot(xn_scr[...], w_fold_ref[:, cs]) + b_fold_ref[:, cs]

    row = lax.broadcasted_iota(jnp.int32, (HALO, gw), 0)
    pos1 = seq_idx * t + row + 1
    for gi, w in enumerate(POOL_WINDOWS):
        cols = slice(gi * gw, (gi + 1) * gw)
        src, k, lo, slot = p_scr.at[:, cols], 1, 8, 0
        while 2 * k < w:
            dst = s_scr.at[slot]
            dst[lo:HALO + t, :] = src[lo:HALO + t, :] + src[lo - k:HALO + t - k, :]
            src, k, lo, slot = dst, 2 * k, lo + 8, 1 - slot
        win = src[HALO:HALO + t, :] + src[HALO - k:HALO + t - k, :]
        cur = p_scr[HALO:HALO + t, cols]
        m_scr[:, cols] = win * (1.0 / w) - cur
        cnt = jnp.minimum(pos1, w).astype(_F32)
        m_scr[0:HALO, cols] = win[0:HALO] / cnt - cur[0:HALO]

    for c in range(0, d, N_CHUNK):
        gv_scr[:, c:c + N_CHUNK] = _gelu(proj(2 * d + c))
    for r in range(0, t, ROW_CHUNK):
        gv = gv_scr[r:r + ROW_CHUNK, :]
        mu = jnp.mean(gv, axis=-1, keepdims=True)
        xc = gv - mu
        var = jnp.mean(xc * xc, axis=-1, keepdims=True)
        vn = xc * lax.rsqrt(var + EPS) * ln_g_ref[...] + ln_b_ref[...]
        vn_scr[r:r + ROW_CHUNK, :] = vn.astype(_BF16)
    for c in range(0, d, N_CHUNK):
        u_scr[:, c:c + N_CHUNK] = _gelu(proj(d + c), scale=0.5)

    qi = lax.broadcasted_iota(jnp.int32, (SGU_BLOCK, SGU_BLOCK), 0) // CHUNK
    kj = lax.broadcasted_iota(jnp.int32, (SGU_BLOCK, SGU_BLOCK), 1) // CHUNK
    causal = qi >= kj
    for h in range(n_groups):
        cols = slice(h * gw, (h + 1) * gw)
        ws = jnp.where(causal, ws_ref[h], 0.0).astype(_BF16)
        for r in range(0, t, SGU_BLOCK):
            rows = slice(r, r + SGU_BLOCK)
            sv = _dot(ws, vn_scr[rows, cols]) + bsp_ref[h]
            gated_scr[rows, cols] = (u_scr[rows, cols] * sv).astype(_BF16)
    for h in range(n_groups):
        cols = slice(h * gw, (h + 1) * gw)
        bb_scr[:, cols] = _dot(gated_scr[:, cols], w_sgu_ref[cols, :])

    for c in range(0, d, N_CHUNK):
        cs = slice(c, c + N_CHUNK)
        ta = jnp.tanh(proj(3 * d + c, scale=0.5))
        tb = jnp.tanh(proj(4 * d + c, scale=0.5))
        ha = m_scr[:, cs]
        hb = bb_scr[:, cs]
        merged_scr[:, cs] = ((ha + hb) + (ha * ta + hb * tb)).astype(_BF16)

    _project_norm_residual(merged_scr, w_out_ref, y_scr, x_tile, g_post_ref, o_ref.at[0])


def _ffn_kernel(h_ref, g_pre_ref, g_post_ref, w1_hbm, w2_hbm, o_ref,
                w1_ref, w2_ref, stage, sem, hn_scr, f_scr, y_scr):
    d_ff = w1_ref.shape[1]

    @pl.when(pl.program_id(0) == 0)
    def _():
        jobs = _cast_jobs(w1_hbm, w1_ref, row_gain_ref=g_pre_ref) + _cast_jobs(w2_hbm, w2_ref)
        _run_weight_jobs(jobs, stage, sem)

    groups = [pl.ds(r0, SEQ_TILE) for r0 in range(0, h_ref.shape[0], SEQ_TILE)]
    for rs in groups:
        hn = hn_scr.at[rs, :]
        _pre_norm(h_ref.at[rs, :], hn)
        for c in range(0, d_ff, N_CHUNK):
            f = jnp.maximum(_dot(hn[...], w1_ref[:, c:c + N_CHUNK]), 0.0)
            f_scr[rs, c:c + N_CHUNK] = (f * f).astype(_BF16)
    for rs in groups:
        _project_norm_residual(f_scr.at[rs, :], w2_ref, y_scr.at[rs, :], h_ref.at[rs, :], g_post_ref,
                               o_ref.at[rs, :])


def _resident(shape):
    zeros = (0,) * len(shape)
    return pl.BlockSpec(shape, lambda *_: zeros, pipeline_mode=pl.Buffered(1))


_IN_HBM = pl.BlockSpec(memory_space=pltpu.HBM)


def kernel(x, norm1_pre_g, w_in, b_in, w_pool, pool_scale, sgu_ln_g, sgu_ln_b, w_spatial, b_spatial,
           w_sgu_proj, w_out, norm1_post_g, norm2_pre_g, w_ff1, w_ff2, norm2_post_g):
    bsz, s_len, d = x.shape
    d_in = w_in.shape[1]
    d_ff = w_ff1.shape[1]
    n_heads, blk, _ = w_spatial.shape
    gw = d // n_heads
    t = SEQ_TILE
    assert s_len % t == 0 and t % SGU_BLOCK == 0 and blk == SGU_BLOCK
    assert d % N_CHUNK == 0 and d_ff % N_CHUNK == 0 and d_in == 5 * d
    assert d % W_STAGE_ROWS == 0 and d_ff % W_STAGE_ROWS == 0
    assert d == W_STAGE_COLS and d_ff % W_STAGE_COLS == 0 and gw <= W_STAGE_COLS
    assert len(POOL_WINDOWS) == n_heads == w_pool.shape[0] and max(POOL_WINDOWS) <= HALO

    row = lambda v: v.reshape(1, -1).astype(_F32)
    w_in, w_out, w_ff1, w_ff2 = w_in.astype(_F32), w_out.astype(_F32), w_ff1.astype(_F32), w_ff2.astype(_F32)
    w_sgu_rows = w_sgu_proj.astype(_F32).reshape(n_heads * gw, gw)
    stage_shape = (W_STAGE_SLOTS, W_STAGE_ROWS, W_STAGE_COLS)

    mix_vmem_in = (row(norm1_pre_g), row(b_in), w_pool.astype(_F32), row(pool_scale), row(sgu_ln_g),
                   row(sgu_ln_b), w_spatial.astype(_F32), b_spatial.astype(_F32), row(norm1_post_g))
    mix_hbm_in = (w_in, w_sgu_rows, w_out)
    tile_spec = pl.BlockSpec((1, t, d), lambda b, s: (b, s, 0))
    h = pl.pallas_call(
        _mix_kernel,
        grid=(bsz, s_len // t),
        in_specs=[tile_spec] + [_resident(a.shape) for a in mix_vmem_in] + [_IN_HBM] * len(mix_hbm_in),
        out_specs=tile_spec,
        out_shape=jax.ShapeDtypeStruct(x.shape, x.dtype),
        scratch_shapes=[
            pltpu.VMEM((d, d), _BF16),
            pltpu.VMEM((1, d), _F32),
            pltpu.VMEM((d, d_in - d), _BF16),
            pltpu.VMEM(w_sgu_rows.shape, _BF16),
            pltpu.VMEM((d, d), _BF16),
            pltpu.VMEM((n_heads, blk, gw), _F32),
            pltpu.VMEM(stage_shape, _F32),
            pltpu.SemaphoreType.DMA((W_STAGE_SLOTS,)),
            pltpu.VMEM((t, d), _BF16),
            pltpu.VMEM((t + HALO, d), _F32),
            pltpu.VMEM((2, t + HALO, gw), _F32),
            pltpu.VMEM((t, d), _F32),
            pltpu.VMEM((t, d), _BF16),
            pltpu.VMEM((t, d), _F32),
            pltpu.VMEM((t, d), _BF16),
            pltpu.VMEM((t, d), _F32),
            pltpu.VMEM((t, d), _F32),
            pltpu.VMEM((t, d), _BF16),
            pltpu.VMEM((t, d), _F32),
        ],
        compiler_params=pltpu.CompilerParams(
            dimension_semantics=("arbitrary", "arbitrary"),
            vmem_limit_bytes=V7X_VMEM_LIMIT_BYTES),
        name="token_mix",
    )(x, *mix_vmem_in, *mix_hbm_in)

    m = bsz * s_len
    t_ffn = FFN_SUB_TILES * t
    assert m % t_ffn == 0
    row_spec = pl.BlockSpec((t_ffn, d), lambda i: (i, 0))
    ffn_vmem_in = (row(norm2_pre_g), row(norm2_post_g))
    out = pl.pallas_call(
        _ffn_kernel,
        grid=(m // t_ffn,),
        in_specs=[row_spec] + [_resident(a.shape) for a in ffn_vmem_in] + [_IN_HBM] * 2,
        out_specs=row_spec,
        out_shape=jax.ShapeDtypeStruct((m, d), x.dtype),
        scratch_shapes=[
            pltpu.VMEM((d, d_ff), _BF16),
            pltpu.VMEM((d_ff, d), _BF16),
            pltpu.VMEM(stage_shape, _F32),
            pltpu.SemaphoreType.DMA((W_STAGE_SLOTS,)),
            pltpu.VMEM((t_ffn, d), _BF16),
            pltpu.VMEM((t_ffn, d_ff), _BF16),
            pltpu.VMEM((t_ffn, d), _F32),
        ],
        compiler_params=pltpu.CompilerParams(
            dimension_semantics=("arbitrary",),
            vmem_limit_bytes=V7X_VMEM_LIMIT_BYTES),
        name="channel_mix",
    )(h.reshape(m, d), *ffn_vmem_in, w_ff1, w_ff2)
    return out.reshape(bsz, s_len, d)
```

```python
import math

import jax
import jax.numpy as jnp
from jax import lax
from jax.experimental import pallas as pl
from jax.experimental.pallas import tpu as pltpu

EPS = 1e-6
CHUNK = 64
POOL_WINDOWS = (2, 4, 8, 16)
SGU_BLOCK = 128
HALO = 32

SEQ_TILE = 512
FFN_SUB_TILES = 2
N_CHUNK = 512
ROW_CHUNK = 32
W_STAGE_ROWS = 512
W_STAGE_COLS = 1024
W_STAGE_SLOTS = 4
V7X_VMEM_LIMIT_BYTES = 56 * 1024 * 1024

_GELU_C = math.sqrt(2.0 / math.pi)
_BF16 = jnp.bfloat16
_F32 = jnp.float32


def _gelu(x, scale=1.0):
    inner = x * (_GELU_C + (0.044715 * _GELU_C) * (x * x))
    hx = (0.5 * scale) * x
    return hx + hx * jnp.tanh(inner)


def _dot(a, b):
    return jnp.dot(a, b, preferred_element_type=_F32)


def _dot_exact(a, b):
    a_hi, b_hi = a.astype(_BF16), b.astype(_BF16)
    a_lo = (a - a_hi.astype(_F32)).astype(_BF16)
    b_lo = (b - b_hi.astype(_F32)).astype(_BF16)
    return _dot(a_hi, b_hi) + (_dot(a_hi, b_lo) + _dot(a_lo, b_hi))


def _pre_norm(src_ref, dst_ref):
    for r in range(0, src_ref.shape[0], ROW_CHUNK):
        rows = slice(r, r + ROW_CHUNK)
        x = src_ref[rows, :]
        ms = jnp.mean(x * x, axis=-1, keepdims=True)
        dst_ref[rows, :] = (x * lax.rsqrt(ms + EPS)).astype(_BF16)


def _project_norm_residual(lhs_ref, w_ref, y_scr, res_ref, g_ref, o_ref):
    t, d = y_scr.shape
    for c in range(0, d, N_CHUNK):
        y_scr[:, c:c + N_CHUNK] = _dot(lhs_ref[...], w_ref[:, c:c + N_CHUNK])
    for r in range(0, t, ROW_CHUNK):
        rows = slice(r, r + ROW_CHUNK)
        y = y_scr[rows, :]
        ms = jnp.mean(y * y, axis=-1, keepdims=True)
        o_ref[rows, :] = res_ref[rows, :] + y * lax.rsqrt(ms + EPS) * g_ref[...]


def _column(row):
    return jnp.transpose(jnp.broadcast_to(row, (128, row.shape[1])))[:, 0:1]


def _cast_jobs(src_hbm, dst_vmem, row_gain_ref=None, col_block_scale=None):
    n_rows, n_cols = dst_vmem.shape
    br, bc = min(n_rows, W_STAGE_ROWS), min(n_cols, W_STAGE_COLS)
    jobs = []
    for r in range(0, n_rows, br):
        for c in range(0, n_cols, bc):
            def finish(stage_view, r=r, c=c):
                w = stage_view[...]
                if row_gain_ref is not None:
                    w = w * _column(row_gain_ref[:, r:r + br])
                if col_block_scale is not None and col_block_scale(c) != 1.0:
                    w = w * col_block_scale(c)
                dst_vmem[r:r + br, c:c + bc] = w.astype(_BF16)
            jobs.append((src_hbm.at[pl.ds(r, br), pl.ds(c, bc)], (br, bc), finish))
    return jobs


def _run_weight_jobs(jobs, stage, sem):
    def fetch(k):
        src, (br, bc), _ = jobs[k]
        slot = k % W_STAGE_SLOTS
        return pltpu.make_async_copy(src, stage.at[slot, pl.ds(0, br), pl.ds(0, bc)], sem.at[slot])

    ahead = W_STAGE_SLOTS - 1
    for k in range(min(ahead, len(jobs))):
        fetch(k).start()
    for k, (_, (br, bc), finish) in enumerate(jobs):
        if k + ahead < len(jobs):
            fetch(k + ahead).start()
        fetch(k).wait()
        finish(stage.at[k % W_STAGE_SLOTS, pl.ds(0, br), pl.ds(0, bc)])


def _mix_kernel(x_ref, g_pre_ref, b_in_ref, w_pool_ref, pscale_ref, ln_g_ref, ln_b_ref, ws_ref, b_sp_ref,
                g_post_ref, w_in_hbm, w_sgu_hbm, w_out_hbm, o_ref,
                w_fold_ref, b_fold_ref, w_in_ref, w_sgu_ref, w_out_ref, bsp_ref, stage, sem,
                xn_scr, p_scr, s_scr, zv_scr, vn_scr, zu_scr, gated_scr, m_scr, bb_scr, merged_scr, y_scr):
    t, d = x_ref.shape[1], x_ref.shape[2]
    n_groups = len(POOL_WINDOWS)
    gw = d // n_groups
    seq_idx = pl.program_id(1)
    x_tile = x_ref.at[0]

    @pl.when(jnp.logical_and(pl.program_id(0) == 0, seq_idx == 0))
    def _():
        def fold_rows(stage_view, r):
            gain = _column(g_pre_ref[:, r:r + W_STAGE_ROWS])
            for gi in range(n_groups):
                cols = slice(gi * gw, (gi + 1) * gw)
                wf = _dot_exact(stage_view[:, cols] * gain, w_pool_ref[gi]) * (0.5 * pscale_ref[:, cols])
                w_fold_ref[r:r + W_STAGE_ROWS, cols] = wf.astype(_BF16)

        jobs = [(w_in_hbm.at[pl.ds(r, W_STAGE_ROWS), pl.ds(0, d)], (W_STAGE_ROWS, d),
                 lambda view, r=r: fold_rows(view, r)) for r in range(0, d, W_STAGE_ROWS)]
        jobs += _cast_jobs(w_in_hbm.at[:, pl.ds(d, w_in_ref.shape[1])], w_in_ref, row_gain_ref=g_pre_ref,
                           col_block_scale=lambda c: 0.5 if c >= 2 * d else 1.0)
        jobs += _cast_jobs(w_sgu_hbm, w_sgu_ref) + _cast_jobs(w_out_hbm, w_out_ref)
        _run_weight_jobs(jobs, stage, sem)
        for gi in range(n_groups):
            cols = slice(gi * gw, (gi + 1) * gw)
            b8 = jnp.broadcast_to(b_in_ref[:, cols], (8, gw))
            b_fold_ref[:, cols] = _dot_exact(b8, w_pool_ref[gi])[0:1] * (0.5 * pscale_ref[:, cols])
        for h in range(n_groups):
            bsp_ref[h] = jnp.broadcast_to(_column(b_sp_ref[h:h + 1, :]), bsp_ref.shape[1:])

    @pl.when(seq_idx == 0)
    def _():
        p_scr[0:HALO, :] = jnp.zeros((HALO, d), _F32)

    @pl.when(seq_idx > 0)
    def _():
        p_scr[0:HALO, :] = p_scr[t:t + HALO, :]

    _pre_norm(x_tile, xn_scr)

    def proj(col0, scale=1.0):
        bias = b_in_ref[:, col0:col0 + N_CHUNK]
        return _dot(xn_scr[...], w_in_ref[:, col0 - d:col0 - d + N_CHUNK]) + (bias if scale == 1.0 else scale * bias)

    for c in range(0, d, N_CHUNK):
        zu_scr[:, c:c + N_CHUNK] = proj(d + c)
    half = t // 2
    for r0 in range(0, t, half):
        for c in range(0, d, N_CHUNK):
            z = _dot(xn_scr[r0:r0 + half, :], w_in_ref[:, d + c:d + c + N_CHUNK])
            zv_scr[r0:r0 + half, c:c + N_CHUNK] = z + b_in_ref[:, 2 * d + c:2 * d + c + N_CHUNK]
    for r in range(0, t, ROW_CHUNK):
        gv = _gelu(zv_scr[r:r + ROW_CHUNK, :])
        mu = jnp.mean(gv, axis=-1, keepdims=True)
        xc = gv - mu
        var = jnp.mean(xc * xc, axis=-1, keepdims=True)
        vn = xc * lax.rsqrt(var + EPS) * ln_g_ref[...] + ln_b_ref[...]
        vn_scr[r:r + ROW_CHUNK, :] = vn.astype(_BF16)

    for c in range(0, d, N_CHUNK):
        cs = slice(c, c + N_CHUNK)
        p_scr[HALO:HALO + t, cs] = _dot(xn_scr[...], w_fold_ref[:, cs]) + b_fold_ref[:, cs]

    row = lax.broadcasted_iota(jnp.int32, (HALO, gw), 0)
    pos1 = seq_idx * t + row + 1
    for gi, w in enumerate(POOL_WINDOWS):
        cols = slice(gi * gw, (gi + 1) * gw)
        src, k, lo, slot = p_scr.at[:, cols], 1, 8, 0
        while 2 * k < w:
            dst = s_scr.at[slot]
            dst[lo:HALO + t, :] = src[lo:HALO + t, :] + src[lo - k:HALO + t - k, :]
            src, k, lo, slot = dst, 2 * k, lo + 8, 1 - slot
        win = src[HALO:HALO + t, :] + src[HALO - k:HALO + t - k, :]
        cur = p_scr[HALO:HALO + t, cols]
        m_scr[:, cols] = win * (1.0 / w) - cur
        cnt = jnp.minimum(pos1, w).astype(_F32)
        m_scr[0:HALO, cols] = win[0:HALO] / cnt - cur[0:HALO]

    qi = lax.broadcasted_iota(jnp.int32, (SGU_BLOCK, SGU_BLOCK), 0) // CHUNK
    kj = lax.broadcasted_iota(jnp.int32, (SGU_BLOCK, SGU_BLOCK), 1) // CHUNK
    causal = qi >= kj
    for h in range(n_groups):
        cols = slice(h * gw, (h + 1) * gw)
        ws = jnp.where(causal, ws_ref[h], 0.0).astype(_BF16)
        for r in range(0, t, SGU_BLOCK):
            rows = slice(r, r + SGU_BLOCK)
            sv = _dot(ws, vn_scr[rows, cols]) + bsp_ref[h]
            gated_scr[rows, cols] = (_gelu(zu_scr[rows, cols], scale=0.5) * sv).astype(_BF16)
    for h in range(n_groups):
        cols = slice(h * gw, (h + 1) * gw)
        bb_scr[:, cols] = _dot(gated_scr[:, cols], w_sgu_ref[cols, :])

    for c in range(0, d, N_CHUNK):
        cs = slice(c, c + N_CHUNK)
        ta = jnp.tanh(proj(3 * d + c, scale=0.5))
        tb = jnp.tanh(proj(4 * d + c, scale=0.5))
        ha = m_scr[:, cs]
        hb = bb_scr[:, cs]
        merged_scr[:, cs] = ((ha + hb) + (ha * ta + hb * tb)).astype(_BF16)

    _project_norm_residual(merged_scr, w_out_ref, y_scr, x_tile, g_post_ref, o_ref.at[0])


def _ffn_kernel(h_ref, g_pre_ref, g_post_ref, w1_hbm, w2_hbm, o_ref,
                w1_ref, w2_ref, stage, sem, hn_scr, f_scr, y_scr):
    d_ff = w1_ref.shape[1]

    @pl.when(pl.program_id(0) == 0)
    def _():
        jobs = _cast_jobs(w1_hbm, w1_ref, row_gain_ref=g_pre_ref) + _cast_jobs(w2_hbm, w2_ref)
        _run_weight_jobs(jobs, stage, sem)

    groups = [pl.ds(r0, SEQ_TILE) for r0 in range(0, h_ref.shape[0], SEQ_TILE)]
    for rs in groups:
        hn = hn_scr.at[rs, :]
        _pre_norm(h_ref.at[rs, :], hn)
        for c in range(0, d_ff, N_CHUNK):
            f = jnp.maximum(_dot(hn[...], w1_ref[:, c:c + N_CHUNK]), 0.0)
            f_scr[rs, c:c + N_CHUNK] = (f * f).astype(_BF16)
    for rs in groups:
        _project_norm_residual(f_scr.at[rs, :], w2_ref, y_scr.at[rs, :], h_ref.at[rs, :], g_post_ref,
                               o_ref.at[rs, :])


def _resident(shape):
    zeros = (0,) * len(shape)
    return pl.BlockSpec(shape, lambda *_: zeros, pipeline_mode=pl.Buffered(1))


_IN_HBM = pl.BlockSpec(memory_space=pltpu.HBM)


def kernel(x, norm1_pre_g, w_in, b_in, w_pool, pool_scale, sgu_ln_g, sgu_ln_b, w_spatial, b_spatial,
           w_sgu_proj, w_out, norm1_post_g, norm2_pre_g, w_ff1, w_ff2, norm2_post_g):
    bsz, s_len, d = x.shape
    d_in = w_in.shape[1]
    d_ff = w_ff1.shape[1]
    n_heads, blk, _ = w_spatial.shape
    gw = d // n_heads
    t = SEQ_TILE
    assert s_len % t == 0 and t % SGU_BLOCK == 0 and blk == SGU_BLOCK
    assert d % N_CHUNK == 0 and d_ff % N_CHUNK == 0 and d_in == 5 * d
    assert d % W_STAGE_ROWS == 0 and d_ff % W_STAGE_ROWS == 0
    assert d == W_STAGE_COLS and d_ff % W_STAGE_COLS == 0 and gw <= W_STAGE_COLS
    assert len(POOL_WINDOWS) == n_heads == w_pool.shape[0] and max(POOL_WINDOWS) <= HALO

    row = lambda v: v.reshape(1, -1).astype(_F32)
    w_in, w_out, w_ff1, w_ff2 = w_in.astype(_F32), w_out.astype(_F32), w_ff1.astype(_F32), w_ff2.astype(_F32)
    w_sgu_rows = w_sgu_proj.astype(_F32).reshape(n_heads * gw, gw)
    stage_shape = (W_STAGE_SLOTS, W_STAGE_ROWS, W_STAGE_COLS)

    mix_vmem_in = (row(norm1_pre_g), row(b_in), w_pool.astype(_F32), row(pool_scale), row(sgu_ln_g),
                   row(sgu_ln_b), w_spatial.astype(_F32), b_spatial.astype(_F32), row(norm1_post_g))
    mix_hbm_in = (w_in, w_sgu_rows, w_out)
    tile_spec = pl.BlockSpec((1, t, d), lambda b, s: (b, s, 0))
    h = pl.pallas_call(
        _mix_kernel,
        grid=(bsz, s_len // t),
        in_specs=[tile_spec] + [_resident(a.shape) for a in mix_vmem_in] + [_IN_HBM] * len(mix_hbm_in),
        out_specs=tile_spec,
        out_shape=jax.ShapeDtypeStruct(x.shape, x.dtype),
        scratch_shapes=[
            pltpu.VMEM((d, d), _BF16),
            pltpu.VMEM((1, d), _F32),
            pltpu.VMEM((d, d_in - d), _BF16),
            pltpu.VMEM(w_sgu_rows.shape, _BF16),
            pltpu.VMEM((d, d), _BF16),
            pltpu.VMEM((n_heads, blk, gw), _F32),
            pltpu.VMEM(stage_shape, _F32),
            pltpu.SemaphoreType.DMA((W_STAGE_SLOTS,)),
            pltpu.VMEM((t, d), _BF16),
            pltpu.VMEM((t + HALO, d), _F32),
            pltpu.VMEM((2, t + HALO, gw), _F32),
            pltpu.VMEM((t, d), _F32),
            pltpu.VMEM((t, d), _BF16),
            pltpu.VMEM((t, d), _F32),
            pltpu.VMEM((t, d), _BF16),
            pltpu.VMEM((t, d), _F32),
            pltpu.VMEM((t, d), _F32),
            pltpu.VMEM((t, d), _BF16),
            pltpu.VMEM((t, d), _F32),
        ],
        compiler_params=pltpu.CompilerParams(
            dimension_semantics=("arbitrary", "arbitrary"),
            vmem_limit_bytes=V7X_VMEM_LIMIT_BYTES),
        name="token_mix",
    )(x, *mix_vmem_in, *mix_hbm_in)

    m = bsz * s_len
    t_ffn = FFN_SUB_TILES * t
    assert m % t_ffn == 0
    row_spec = pl.BlockSpec((t_ffn, d), lambda i: (i, 0))
    ffn_vmem_in = (row(norm2_pre_g), row(norm2_post_g))
    out = pl.pallas_call(
        _ffn_kernel,
        grid=(m // t_ffn,),
        in_specs=[row_spec] + [_resident(a.shape) for a in ffn_vmem_in] + [_IN_HBM] * 2,
        out_specs=row_spec,
        out_shape=jax.ShapeDtypeStruct((m, d), x.dtype),
        scratch_shapes=[
            pltpu.VMEM((d, d_ff), _BF16),
            pltpu.VMEM((d_ff, d), _BF16),
            pltpu.VMEM(stage_shape, _F32),
            pltpu.SemaphoreType.DMA((W_STAGE_SLOTS,)),
            pltpu.VMEM((t_ffn, d), _BF16),
            pltpu.VMEM((t_ffn, d_ff), _BF16),
            pltpu.VMEM((t_ffn, d), _F32),
        ],
        compiler_params=pltpu.CompilerParams(
            dimension_semantics=("arbitrary",),
            vmem_limit_bytes=V7X_VMEM_LIMIT_BYTES),
        name="channel_mix",
    )(h.reshape(m, d), *ffn_vmem_in, w_ff1, w_ff2)
    return out.reshape(bsz, s_len, d)
```

```python
import math

import jax
import jax.numpy as jnp
from jax import lax
from jax.experimental import pallas as pl
from jax.experimental.pallas import tpu as pltpu

EPS = 1e-6
CHUNK = 64
POOL_WINDOWS = (2, 4, 8, 16)
SGU_BLOCK = 128
HALO = 32

SEQ_TILE = 512
FFN_SUB_TILES = 2
N_CHUNK = 512
ROW_CHUNK = 32
W_STAGE_ROWS = 512
W_STAGE_COLS = 1024
W_STAGE_SLOTS = 4
V7X_VMEM_LIMIT_BYTES = 56 * 1024 * 1024

_GELU_C = math.sqrt(2.0 / math.pi)
_BF16 = jnp.bfloat16
_F32 = jnp.float32


def _gelu(x, scale=1.0):
    inner = x * (_GELU_C + (0.044715 * _GELU_C) * (x * x))
    hx = (0.5 * scale) * x
    return hx + hx * jnp.tanh(inner)


def _dot(a, b):
    return jnp.dot(a, b, preferred_element_type=_F32)


def _dot_exact(a, b):
    a_hi, b_hi = a.astype(_BF16), b.astype(_BF16)
    a_lo = (a - a_hi.astype(_F32)).astype(_BF16)
    b_lo = (b - b_hi.astype(_F32)).astype(_BF16)
    return _dot(a_hi, b_hi) + (_dot(a_hi, b_lo) + _dot(a_lo, b_hi))


def _pre_norm(src_ref, dst_ref):
    for r in range(0, src_ref.shape[0], ROW_CHUNK):
        rows = slice(r, r + ROW_CHUNK)
        x = src_ref[rows, :]
        ms = jnp.mean(x * x, axis=-1, keepdims=True)
        dst_ref[rows, :] = (x * lax.rsqrt(ms + EPS)).astype(_BF16)


def _project_norm_residual(lhs_ref, w_ref, y_scr, res_ref, g_ref, o_ref):
    t, d = y_scr.shape
    for c in range(0, d, N_CHUNK):
        y_scr[:, c:c + N_CHUNK] = _dot(lhs_ref[...], w_ref[:, c:c + N_CHUNK])
    for r in range(0, t, ROW_CHUNK):
        rows = slice(r, r + ROW_CHUNK)
        y = y_scr[rows, :]
        ms = jnp.mean(y * y, axis=-1, keepdims=True)
        o_ref[rows, :] = res_ref[rows, :] + y * lax.rsqrt(ms + EPS) * g_ref[...]


def _column(row):
    return jnp.transpose(jnp.broadcast_to(row, (128, row.shape[1])))[:, 0:1]


def _cast_jobs(src_hbm, dst_vmem, row_gain_ref=None, col_block_scale=None):
    n_rows, n_cols = dst_vmem.shape
    br, bc = min(n_rows, W_STAGE_ROWS), min(n_cols, W_STAGE_COLS)
    jobs = []
    for r in range(0, n_rows, br):
        for c in range(0, n_cols, bc):
            def finish(stage_view, r=r, c=c):
                w = stage_view[...]
                if row_gain_ref is not None:
                    w = w * _column(row_gain_ref[:, r:r + br])
                if col_block_scale is not None and col_block_scale(c) != 1.0:
                    w = w * col_block_scale(c)
                dst_vmem[r:r + br, c:c + bc] = w.astype(_BF16)
            jobs.append((src_hbm.at[pl.ds(r, br), pl.ds(c, bc)], (br, bc), finish))
    return jobs


def _run_weight_jobs(jobs, stage, sem):
    def fetch(k):
        src, (br, bc), _ = jobs[k]
        slot = k % W_STAGE_SLOTS
        return pltpu.make_async_copy(src, stage.at[slot, pl.ds(0, br), pl.ds(0, bc)], sem.at[slot])

    ahead = W_STAGE_SLOTS - 1
    for k in range(min(ahead, len(jobs))):
        fetch(k).start()
    for k, (_, (br, bc), finish) in enumerate(jobs):
        if k + ahead < len(jobs):
            fetch(k + ahead).start()
        fetch(k).wait()
        finish(stage.at[k % W_STAGE_SLOTS, pl.ds(0, br), pl.ds(0, bc)])


def _mix_kernel(x_ref, g_pre_ref, b_in_ref, w_pool_ref, pscale_ref, ln_g_ref, ln_b_ref, ws_ref, b_sp_ref,
                g_post_ref, w_in_hbm, w_sgu_hbm, w_out_hbm, o_ref,
                w_fold_ref, b_fold_ref, w_in_ref, w_sgu_ref, w_out_ref, bsp_ref, stage, sem,
                xn_scr, p_scr, s_scr, zv_scr, vn_scr, zu_scr, gated_scr, m_scr, bb_scr, merged_scr, y_scr):
    t, d = x_ref.shape[1], x_ref.shape[2]
    n_groups = len(POOL_WINDOWS)
    gw = d // n_groups
    seq_idx = pl.program_id(1)
    x_tile = x_ref.at[0]

    @pl.when(jnp.logical_and(pl.program_id(0) == 0, seq_idx == 0))
    def _():
        def fold_rows(stage_view, r):
            gain = _column(g_pre_ref[:, r:r + W_STAGE_ROWS])
            for gi in range(n_groups):
                cols = slice(gi * gw, (gi + 1) * gw)
                wf = _dot_exact(stage_view[:, cols] * gain, w_pool_ref[gi]) * (0.5 * pscale_ref[:, cols])
                w_fold_ref[r:r + W_STAGE_ROWS, cols] = wf.astype(_BF16)

        jobs = [(w_in_hbm.at[pl.ds(r, W_STAGE_ROWS), pl.ds(0, d)], (W_STAGE_ROWS, d),
                 lambda view, r=r: fold_rows(view, r)) for r in range(0, d, W_STAGE_ROWS)]
        jobs += _cast_jobs(w_in_hbm.at[:, pl.ds(d, w_in_ref.shape[1])], w_in_ref, row_gain_ref=g_pre_ref,
                           col_block_scale=lambda c: 0.5 if c >= 2 * d else 1.0)
        jobs += _cast_jobs(w_sgu_hbm, w_sgu_ref) + _cast_jobs(w_out_hbm, w_out_ref)
        _run_weight_jobs(jobs, stage, sem)
        for gi in range(n_groups):
            cols = slice(gi * gw, (gi + 1) * gw)
            b8 = jnp.broadcast_to(b_in_ref[:, cols], (8, gw))
            b_fold_ref[:, cols] = _dot_exact(b8, w_pool_ref[gi])[0:1] * (0.5 * pscale_ref[:, cols])
        for h in range(n_groups):
            bsp_ref[h] = jnp.broadcast_to(_column(b_sp_ref[h:h + 1, :]), bsp_ref.shape[1:])

    @pl.when(seq_idx == 0)
    def _():
        p_scr[0:HALO, :] = jnp.zeros((HALO, d), _F32)

    @pl.when(seq_idx > 0)
    def _():
        p_scr[0:HALO, :] = p_scr[t:t + HALO, :]

    _pre_norm(x_tile, xn_scr)

    def proj(col0, scale=1.0):
        bias = b_in_ref[:, col0:col0 + N_CHUNK]
        return _dot(xn_scr[...], w_in_ref[:, col0 - d:col0 - d + N_CHUNK]) + (bias if scale == 1.0 else scale * bias)

    for c in range(0, d, N_CHUNK):
        zu_scr[:, c:c + N_CHUNK] = proj(d + c)
    for c in range(0, d, N_CHUNK):
        zv_scr[:, c:c + N_CHUNK] = proj(2 * d + c)
    for r in range(0, t, ROW_CHUNK):
        gv = _gelu(zv_scr[r:r + ROW_CHUNK, :])
        mu = jnp.mean(gv, axis=-1, keepdims=True)
        xc = gv - mu
        var = jnp.mean(xc * xc, axis=-1, keepdims=True)
        vn = xc * lax.rsqrt(var + EPS) * ln_g_ref[...] + ln_b_ref[...]
        vn_scr[r:r + ROW_CHUNK, :] = vn.astype(_BF16)

    for c in range(0, d, N_CHUNK):
        cs = slice(c, c + N_CHUNK)
        p_scr[HALO:HALO + t, cs] = _dot(xn_scr[...], w_fold_ref[:, cs]) + b_fold_ref[:, cs]

    row = lax.broadcasted_iota(jnp.int32, (HALO, gw), 0)
    pos1 = seq_idx * t + row + 1
    for gi, w in enumerate(POOL_WINDOWS):
        cols = slice(gi * gw, (gi + 1) * gw)
        src, k, lo, slot = p_scr.at[:, cols], 1, 8, 0
        while 2 * k < w:
            dst = s_scr.at[slot]
            dst[lo:HALO + t, :] = src[lo:HALO + t, :] + src[lo - k:HALO + t - k, :]
            src, k, lo, slot = dst, 2 * k, lo + 8, 1 - slot
        win = src[HALO:HALO + t, :] + src[HALO - k:HALO + t - k, :]
        cur = p_scr[HALO:HALO + t, cols]
        m_scr[:, cols] = win * (1.0 / w) - cur
        cnt = jnp.minimum(pos1, w).astype(_F32)
        m_scr[0:HALO, cols] = win[0:HALO] / cnt - cur[0:HALO]

    qi = lax.broadcasted_iota(jnp.int32, (SGU_BLOCK, SGU_BLOCK), 0) // CHUNK
    kj = lax.broadcasted_iota(jnp.int32, (SGU_BLOCK, SGU_BLOCK), 1) // CHUNK
    causal = qi >= kj
    for h in range(n_groups):
        cols = slice(h * gw, (h + 1) * gw)
        ws = jnp.where(causal, ws_ref[h], 0.0).astype(_BF16)
        for r in range(0, t, SGU_BLOCK):
            rows = slice(r, r + SGU_BLOCK)
            sv = _dot(ws, vn_scr[rows, cols]) + bsp_ref[h]
            gated_scr[rows, cols] = (_gelu(zu_scr[rows, cols], scale=0.5) * sv).astype(_BF16)
    for h in range(n_groups):
        cols = slice(h * gw, (h + 1) * gw)
        bb_scr[:, cols] = _dot(gated_scr[:, cols], w_sgu_ref[cols, :])

    for c in range(0, d, N_CHUNK):
        cs = slice(c, c + N_CHUNK)
        ta = jnp.tanh(proj(3 * d + c, scale=0.5))
        tb = jnp.tanh(proj(4 * d + c, scale=0.5))
        ha = m_scr[:, cs]
        hb = bb_scr[:, cs]
        merged_scr[:, cs] = ((ha + hb) + (ha * ta + hb * tb)).astype(_BF16)

    _project_norm_residual(merged_scr, w_out_ref, y_scr, x_tile, g_post_ref, o_ref.at[0])


def _ffn_kernel(h_hbm, g_pre_ref, g_post_ref, w1_hbm, w2_hbm, o_hbm,
                w1_ref, w2_ref, stage, sem, hn_scr, f_scr, y_scr):
    d_ff = w1_ref.shape[1]
    t_ffn, d = y_scr.shape

    jobs = _cast_jobs(w1_hbm, w1_ref, row_gain_ref=g_pre_ref) + _cast_jobs(w2_hbm, w2_ref)
    _run_weight_jobs(jobs, stage, sem)

    def rows_step(h_ref, o_ref):
        groups = [pl.ds(r0, SEQ_TILE) for r0 in range(0, t_ffn, SEQ_TILE)]
        for rs in groups:
            hn = hn_scr.at[rs, :]
            _pre_norm(h_ref.at[rs, :], hn)
            for c in range(0, d_ff, N_CHUNK):
                f = jnp.maximum(_dot(hn[...], w1_ref[:, c:c + N_CHUNK]), 0.0)
                f_scr[rs, c:c + N_CHUNK] = (f * f).astype(_BF16)
        for rs in groups:
            _project_norm_residual(f_scr.at[rs, :], w2_ref, y_scr.at[rs, :], h_ref.at[rs, :], g_post_ref,
                                   o_ref.at[rs, :])

    row_spec = pl.BlockSpec((t_ffn, d), lambda i: (i, 0))
    pltpu.emit_pipeline(rows_step, grid=(h_hbm.shape[0] // t_ffn,), in_specs=[row_spec],
                        out_specs=[row_spec])(h_hbm, o_hbm)


def _resident(shape):
    zeros = (0,) * len(shape)
    return pl.BlockSpec(shape, lambda *_: zeros, pipeline_mode=pl.Buffered(1))


_IN_HBM = pl.BlockSpec(memory_space=pltpu.HBM)


def kernel(x, norm1_pre_g, w_in, b_in, w_pool, pool_scale, sgu_ln_g, sgu_ln_b, w_spatial, b_spatial,
           w_sgu_proj, w_out, norm1_post_g, norm2_pre_g, w_ff1, w_ff2, norm2_post_g):
    bsz, s_len, d = x.shape
    d_in = w_in.shape[1]
    d_ff = w_ff1.shape[1]
    n_heads, blk, _ = w_spatial.shape
    gw = d // n_heads
    t = SEQ_TILE
    assert s_len % t == 0 and t % SGU_BLOCK == 0 and blk == SGU_BLOCK
    assert d % N_CHUNK == 0 and d_ff % N_CHUNK == 0 and d_in == 5 * d
    assert d % W_STAGE_ROWS == 0 and d_ff % W_STAGE_ROWS == 0
    assert d == W_STAGE_COLS and d_ff % W_STAGE_COLS == 0 and gw <= W_STAGE_COLS
    assert len(POOL_WINDOWS) == n_heads == w_pool.shape[0] and max(POOL_WINDOWS) <= HALO

    row = lambda v: v.reshape(1, -1).astype(_F32)
    w_in, w_out, w_ff1, w_ff2 = w_in.astype(_F32), w_out.astype(_F32), w_ff1.astype(_F32), w_ff2.astype(_F32)
    w_sgu_rows = w_sgu_proj.astype(_F32).reshape(n_heads * gw, gw)
    stage_shape = (W_STAGE_SLOTS, W_STAGE_ROWS, W_STAGE_COLS)

    mix_vmem_in = (row(norm1_pre_g), row(b_in), w_pool.astype(_F32), row(pool_scale), row(sgu_ln_g),
                   row(sgu_ln_b), w_spatial.astype(_F32), b_spatial.astype(_F32), row(norm1_post_g))
    mix_hbm_in = (w_in, w_sgu_rows, w_out)
    tile_spec = pl.BlockSpec((1, t, d), lambda b, s: (b, s, 0))
    h = pl.pallas_call(
        _mix_kernel,
        grid=(bsz, s_len // t),
        in_specs=[tile_spec] + [_resident(a.shape) for a in mix_vmem_in] + [_IN_HBM] * len(mix_hbm_in),
        out_specs=tile_spec,
        out_shape=jax.ShapeDtypeStruct(x.shape, x.dtype),
        scratch_shapes=[
            pltpu.VMEM((d, d), _BF16),
            pltpu.VMEM((1, d), _F32),
            pltpu.VMEM((d, d_in - d), _BF16),
            pltpu.VMEM(w_sgu_rows.shape, _BF16),
            pltpu.VMEM((d, d), _BF16),
            pltpu.VMEM((n_heads, blk, gw), _F32),
            pltpu.VMEM(stage_shape, _F32),
            pltpu.SemaphoreType.DMA((W_STAGE_SLOTS,)),
            pltpu.VMEM((t, d), _BF16),
            pltpu.VMEM((t + HALO, d), _F32),
            pltpu.VMEM((2, t + HALO, gw), _F32),
            pltpu.VMEM((t, d), _F32),
            pltpu.VMEM((t, d), _BF16),
            pltpu.VMEM((t, d), _F32),
            pltpu.VMEM((t, d), _BF16),
            pltpu.VMEM((t, d), _F32),
            pltpu.VMEM((t, d), _F32),
            pltpu.VMEM((t, d), _BF16),
            pltpu.VMEM((t, d), _F32),
        ],
        compiler_params=pltpu.CompilerParams(
            dimension_semantics=("arbitrary", "arbitrary"),
            vmem_limit_bytes=V7X_VMEM_LIMIT_BYTES),
        name="token_mix",
    )(x, *mix_vmem_in, *mix_hbm_in)

    m = bsz * s_len
    t_ffn = FFN_SUB_TILES * t
    assert m % t_ffn == 0
    ffn_vmem_in = (row(norm2_pre_g), row(norm2_post_g))
    in_vmem = pl.BlockSpec(memory_space=pltpu.VMEM)
    out = pl.pallas_call(
        _ffn_kernel,
        in_specs=[_IN_HBM] + [in_vmem] * len(ffn_vmem_in) + [_IN_HBM] * 2,
        out_specs=_IN_HBM,
        out_shape=jax.ShapeDtypeStruct((m, d), x.dtype),
        scratch_shapes=[
            pltpu.VMEM((d, d_ff), _BF16),
            pltpu.VMEM((d_ff, d), _BF16),
            pltpu.VMEM(stage_shape, _F32),
            pltpu.SemaphoreType.DMA((W_STAGE_SLOTS,)),
            pltpu.VMEM((t_ffn, d), _BF16),
            pltpu.VMEM((t_ffn, d_ff), _BF16),
            pltpu.VMEM((t_ffn, d), _F32),
        ],
        compiler_params=pltpu.CompilerParams(vmem_limit_bytes=V7X_VMEM_LIMIT_BYTES),
        name="channel_mix",
    )(h.reshape(m, d), *ffn_vmem_in, w_ff1, w_ff2)
    return out.reshape(bsz, s_len, d)
```

```python
import math

import jax
import jax.numpy as jnp
from jax import lax
from jax.experimental import pallas as pl
from jax.experimental.pallas import tpu as pltpu

EPS = 1e-6
CHUNK = 64
POOL_WINDOWS = (2, 4, 8, 16)
SGU_BLOCK = 128
HALO = 32

SEQ_TILE = 512
FFN_SUB_TILES = 2
N_CHUNK = 512
ROW_CHUNK = 32
W_STAGE_ROWS = 512
W_STAGE_COLS = 1024
W_STAGE_SLOTS = 4
V7X_VMEM_LIMIT_BYTES = 56 * 1024 * 1024

_GELU_C = math.sqrt(2.0 / math.pi)
_BF16 = jnp.bfloat16
_F32 = jnp.float32


def _gelu(x, scale=1.0):
    inner = x * (_GELU_C + (0.044715 * _GELU_C) * (x * x))
    hx = (0.5 * scale) * x
    return hx + hx * jnp.tanh(inner)


def _dot(a, b):
    return jnp.dot(a, b, preferred_element_type=_F32)


def _dot_exact(a, b):
    a_hi, b_hi = a.astype(_BF16), b.astype(_BF16)
    a_lo = (a - a_hi.astype(_F32)).astype(_BF16)
    b_lo = (b - b_hi.astype(_F32)).astype(_BF16)
    return _dot(a_hi, b_hi) + (_dot(a_hi, b_lo) + _dot(a_lo, b_hi))


def _pre_norm(src_ref, dst_ref):
    for r in range(0, src_ref.shape[0], ROW_CHUNK):
        rows = slice(r, r + ROW_CHUNK)
        x = src_ref[rows, :]
        ms = jnp.mean(x * x, axis=-1, keepdims=True)
        dst_ref[rows, :] = (x * lax.rsqrt(ms + EPS)).astype(_BF16)


def _project_norm_residual(lhs_ref, w_ref, y_scr, res_ref, g_ref, o_ref):
    t, d = y_scr.shape
    for c in range(0, d, N_CHUNK):
        y_scr[:, c:c + N_CHUNK] = _dot(lhs_ref[...], w_ref[:, c:c + N_CHUNK])
    for r in range(0, t, ROW_CHUNK):
        rows = slice(r, r + ROW_CHUNK)
        y = y_scr[rows, :]
        ms = jnp.mean(y * y, axis=-1, keepdims=True)
        o_ref[rows, :] = res_ref[rows, :] + y * lax.rsqrt(ms + EPS) * g_ref[...]


def _column(row):
    return jnp.transpose(jnp.broadcast_to(row, (128, row.shape[1])))[:, 0:1]


def _cast_jobs(src_hbm, dst_vmem, row_gain_ref=None, col_block_scale=None):
    n_rows, n_cols = dst_vmem.shape
    br, bc = min(n_rows, W_STAGE_ROWS), min(n_cols, W_STAGE_COLS)
    jobs = []
    for r in range(0, n_rows, br):
        for c in range(0, n_cols, bc):
            def finish(stage_view, r=r, c=c):
                w = stage_view[...]
                if row_gain_ref is not None:
                    w = w * _column(row_gain_ref[:, r:r + br])
                if col_block_scale is not None and col_block_scale(c) != 1.0:
                    w = w * col_block_scale(c)
                dst_vmem[r:r + br, c:c + bc] = w.astype(_BF16)
            jobs.append((src_hbm.at[pl.ds(r, br), pl.ds(c, bc)], (br, bc), finish))
    return jobs


def _run_weight_jobs(jobs, stage, sem):
    def fetch(k):
        src, (br, bc), _ = jobs[k]
        slot = k % W_STAGE_SLOTS
        return pltpu.make_async_copy(src, stage.at[slot, pl.ds(0, br), pl.ds(0, bc)], sem.at[slot])

    ahead = W_STAGE_SLOTS - 1
    for k in range(min(ahead, len(jobs))):
        fetch(k).start()
    for k, (_, (br, bc), finish) in enumerate(jobs):
        if k + ahead < len(jobs):
            fetch(k + ahead).start()
        fetch(k).wait()
        finish(stage.at[k % W_STAGE_SLOTS, pl.ds(0, br), pl.ds(0, bc)])


def _mix_kernel(x_ref, g_pre_ref, b_in_ref, w_pool_ref, pscale_ref, ln_g_ref, ln_b_ref, ws_ref, b_sp_ref,
                g_post_ref, w_in_hbm, w_sgu_hbm, w_out_hbm, o_ref,
                w_fold_ref, b_fold_ref, w_in_ref, w_sgu_ref, w_out_ref, bsp_ref, stage, sem,
                xn_scr, p_scr, s_scr, zv_scr, vn_scr, zu_scr, gated_scr, m_scr, bb_scr, merged_scr, y_scr):
    t, d = x_ref.shape[1], x_ref.shape[2]
    n_groups = len(POOL_WINDOWS)
    gw = d // n_groups
    seq_idx = pl.program_id(1)
    x_tile = x_ref.at[0]

    @pl.when(jnp.logical_and(pl.program_id(0) == 0, seq_idx == 0))
    def _():
        def fold_rows(stage_view, r):
            gain = _column(g_pre_ref[:, r:r + W_STAGE_ROWS])
            for gi in range(n_groups):
                cols = slice(gi * gw, (gi + 1) * gw)
                wf = _dot_exact(stage_view[:, cols] * gain, w_pool_ref[gi]) * (0.5 * pscale_ref[:, cols])
                w_fold_ref[r:r + W_STAGE_ROWS, cols] = wf.astype(_BF16)

        jobs = [(w_in_hbm.at[pl.ds(r, W_STAGE_ROWS), pl.ds(0, d)], (W_STAGE_ROWS, d),
                 lambda view, r=r: fold_rows(view, r)) for r in range(0, d, W_STAGE_ROWS)]
        jobs += _cast_jobs(w_in_hbm.at[:, pl.ds(d, w_in_ref.shape[1])], w_in_ref, row_gain_ref=g_pre_ref,
                           col_block_scale=lambda c: 0.5 if c >= 2 * d else 1.0)
        jobs += _cast_jobs(w_sgu_hbm, w_sgu_ref) + _cast_jobs(w_out_hbm, w_out_ref)
        _run_weight_jobs(jobs, stage, sem)
        for gi in range(n_groups):
            cols = slice(gi * gw, (gi + 1) * gw)
            b8 = jnp.broadcast_to(b_in_ref[:, cols], (8, gw))
            b_fold_ref[:, cols] = _dot_exact(b8, w_pool_ref[gi])[0:1] * (0.5 * pscale_ref[:, cols])
        for h in range(n_groups):
            bsp_ref[h] = jnp.broadcast_to(_column(b_sp_ref[h:h + 1, :]), bsp_ref.shape[1:])

    @pl.when(seq_idx == 0)
    def _():
        p_scr[0:HALO, :] = jnp.zeros((HALO, d), _F32)

    @pl.when(seq_idx > 0)
    def _():
        p_scr[0:HALO, :] = p_scr[t:t + HALO, :]

    _pre_norm(x_tile, xn_scr)

    def proj(col0, scale=1.0):
        bias = b_in_ref[:, col0:col0 + N_CHUNK]
        return _dot(xn_scr[...], w_in_ref[:, col0 - d:col0 - d + N_CHUNK]) + (bias if scale == 1.0 else scale * bias)

    for c in range(0, d, N_CHUNK):
        zu_scr[:, c:c + N_CHUNK] = proj(d + c)
    for c in range(0, d, N_CHUNK):
        zv_scr[:, c:c + N_CHUNK] = proj(2 * d + c)
    for r in range(0, t, ROW_CHUNK):
        gv = _gelu(zv_scr[r:r + ROW_CHUNK, :])
        mu = jnp.mean(gv, axis=-1, keepdims=True)
        xc = gv - mu
        var = jnp.mean(xc * xc, axis=-1, keepdims=True)
        vn = xc * lax.rsqrt(var + EPS) * ln_g_ref[...] + ln_b_ref[...]
        vn_scr[r:r + ROW_CHUNK, :] = vn.astype(_BF16)

    for c in range(0, d, N_CHUNK):
        cs = slice(c, c + N_CHUNK)
        p_scr[HALO:HALO + t, cs] = _dot(xn_scr[...], w_fold_ref[:, cs]) + b_fold_ref[:, cs]

    row = lax.broadcasted_iota(jnp.int32, (HALO, gw), 0)
    pos1 = seq_idx * t + row + 1
    for gi, w in enumerate(POOL_WINDOWS):
        cols = slice(gi * gw, (gi + 1) * gw)
        src, k, lo, slot = p_scr.at[:, cols], 1, 8, 0
        while 2 * k < w:
            dst = s_scr.at[slot]
            dst[lo:HALO + t, :] = src[lo:HALO + t, :] + src[lo - k:HALO + t - k, :]
            src, k, lo, slot = dst, 2 * k, lo + 8, 1 - slot
        win = src[HALO:HALO + t, :] + src[HALO - k:HALO + t - k, :]
        cur = p_scr[HALO:HALO + t, cols]
        m_scr[:, cols] = win * (1.0 / w) - cur
        cnt = jnp.minimum(pos1, w).astype(_F32)
        m_scr[0:HALO, cols] = win[0:HALO] / cnt - cur[0:HALO]

    qi = lax.broadcasted_iota(jnp.int32, (SGU_BLOCK, SGU_BLOCK), 0) // CHUNK
    kj = lax.broadcasted_iota(jnp.int32, (SGU_BLOCK, SGU_BLOCK), 1) // CHUNK
    causal = qi >= kj
    for h in range(n_groups):
        cols = slice(h * gw, (h + 1) * gw)
        ws = jnp.where(causal, ws_ref[h], 0.0).astype(_BF16)
        for r in range(0, t, SGU_BLOCK):
            rows = slice(r, r + SGU_BLOCK)
            sv = _dot(ws, vn_scr[rows, cols]) + bsp_ref[h]
            gated_scr[rows, cols] = (_gelu(zu_scr[rows, cols], scale=0.5) * sv).astype(_BF16)
    for h in range(n_groups):
        cols = slice(h * gw, (h + 1) * gw)
        bb_scr[:, cols] = _dot(gated_scr[:, cols], w_sgu_ref[cols, :])

    for c in range(0, d, N_CHUNK):
        cs = slice(c, c + N_CHUNK)
        ta = jnp.tanh(proj(3 * d + c, scale=0.5))
        tb = jnp.tanh(proj(4 * d + c, scale=0.5))
        ha = m_scr[:, cs]
        hb = bb_scr[:, cs]
        merged_scr[:, cs] = ((ha + hb) + (ha * ta + hb * tb)).astype(_BF16)

    _project_norm_residual(merged_scr, w_out_ref, y_scr, x_tile, g_post_ref, o_ref.at[0])


def _mix_call(x_hbm, *refs):
    n_vmem_in = 9
    vmem_in, (w_in_hbm, w_sgu_hbm, w_out_hbm, o_hbm), scratch = refs[:n_vmem_in], refs[n_vmem_in:n_vmem_in + 4], \
        refs[n_vmem_in + 4:]
    bsz, s_len, d = x_hbm.shape
    tile_spec = pl.BlockSpec((1, SEQ_TILE, d), lambda b, s: (b, s, 0))

    def tile_step(x_ref, o_ref):
        _mix_kernel(x_ref, *vmem_in, w_in_hbm, w_sgu_hbm, w_out_hbm, o_ref, *scratch)

    pltpu.emit_pipeline(tile_step, grid=(bsz, s_len // SEQ_TILE), in_specs=[tile_spec],
                        out_specs=[tile_spec])(x_hbm, o_hbm)


def _ffn_kernel(h_hbm, g_pre_ref, g_post_ref, w1_hbm, w2_hbm, o_hbm,
                w1_ref, w2_ref, stage, sem, hn_scr, f_scr, y_scr):
    d_ff = w1_ref.shape[1]
    t_ffn, d = y_scr.shape

    jobs = _cast_jobs(w1_hbm, w1_ref, row_gain_ref=g_pre_ref) + _cast_jobs(w2_hbm, w2_ref)
    _run_weight_jobs(jobs, stage, sem)

    def rows_step(h_ref, o_ref):
        groups = [pl.ds(r0, SEQ_TILE) for r0 in range(0, t_ffn, SEQ_TILE)]
        for rs in groups:
            hn = hn_scr.at[rs, :]
            _pre_norm(h_ref.at[rs, :], hn)
            for c in range(0, d_ff, N_CHUNK):
                f = jnp.maximum(_dot(hn[...], w1_ref[:, c:c + N_CHUNK]), 0.0)
                f_scr[rs, c:c + N_CHUNK] = (f * f).astype(_BF16)
        for rs in groups:
            _project_norm_residual(f_scr.at[rs, :], w2_ref, y_scr.at[rs, :], h_ref.at[rs, :], g_post_ref,
                                   o_ref.at[rs, :])

    row_spec = pl.BlockSpec((t_ffn, d), lambda i: (i, 0))
    pltpu.emit_pipeline(rows_step, grid=(h_hbm.shape[0] // t_ffn,), in_specs=[row_spec],
                        out_specs=[row_spec])(h_hbm, o_hbm)


def _resident(shape):
    zeros = (0,) * len(shape)
    return pl.BlockSpec(shape, lambda *_: zeros, pipeline_mode=pl.Buffered(1))


_IN_HBM = pl.BlockSpec(memory_space=pltpu.HBM)


def kernel(x, norm1_pre_g, w_in, b_in, w_pool, pool_scale, sgu_ln_g, sgu_ln_b, w_spatial, b_spatial,
           w_sgu_proj, w_out, norm1_post_g, norm2_pre_g, w_ff1, w_ff2, norm2_post_g):
    bsz, s_len, d = x.shape
    d_in = w_in.shape[1]
    d_ff = w_ff1.shape[1]
    n_heads, blk, _ = w_spatial.shape
    gw = d // n_heads
    t = SEQ_TILE
    assert s_len % t == 0 and t % SGU_BLOCK == 0 and blk == SGU_BLOCK
    assert d % N_CHUNK == 0 and d_ff % N_CHUNK == 0 and d_in == 5 * d
    assert d % W_STAGE_ROWS == 0 and d_ff % W_STAGE_ROWS == 0
    assert d == W_STAGE_COLS and d_ff % W_STAGE_COLS == 0 and gw <= W_STAGE_COLS
    assert len(POOL_WINDOWS) == n_heads == w_pool.shape[0] and max(POOL_WINDOWS) <= HALO

    row = lambda v: v.reshape(1, -1).astype(_F32)
    w_in, w_out, w_ff1, w_ff2 = w_in.astype(_F32), w_out.astype(_F32), w_ff1.astype(_F32), w_ff2.astype(_F32)
    w_sgu_rows = w_sgu_proj.astype(_F32).reshape(n_heads * gw, gw)
    stage_shape = (W_STAGE_SLOTS, W_STAGE_ROWS, W_STAGE_COLS)

    mix_vmem_in = (row(norm1_pre_g), row(b_in), w_pool.astype(_F32), row(pool_scale), row(sgu_ln_g),
                   row(sgu_ln_b), w_spatial.astype(_F32), b_spatial.astype(_F32), row(norm1_post_g))
    mix_hbm_in = (w_in, w_sgu_rows, w_out)
    in_vmem = pl.BlockSpec(memory_space=pltpu.VMEM)
    assert len(mix_vmem_in) == 9
    h = pl.pallas_call(
        _mix_call,
        in_specs=[_IN_HBM] + [in_vmem] * len(mix_vmem_in) + [_IN_HBM] * len(mix_hbm_in),
        out_specs=_IN_HBM,
        out_shape=jax.ShapeDtypeStruct(x.shape, x.dtype),
        scratch_shapes=[
            pltpu.VMEM((d, d), _BF16),
            pltpu.VMEM((1, d), _F32),
            pltpu.VMEM((d, d_in - d), _BF16),
            pltpu.VMEM(w_sgu_rows.shape, _BF16),
            pltpu.VMEM((d, d), _BF16),
            pltpu.VMEM((n_heads, blk, gw), _F32),
            pltpu.VMEM(stage_shape, _F32),
            pltpu.SemaphoreType.DMA((W_STAGE_SLOTS,)),
            pltpu.VMEM((t, d), _BF16),
            pltpu.VMEM((t + HALO, d), _F32),
            pltpu.VMEM((2, t + HALO, gw), _F32),
            pltpu.VMEM((t, d), _F32),
            pltpu.VMEM((t, d), _BF16),
            pltpu.VMEM((t, d), _F32),
            pltpu.VMEM((t, d), _BF16),
            pltpu.VMEM((t, d), _F32),
            pltpu.VMEM((t, d), _F32),
            pltpu.VMEM((t, d), _BF16),
            pltpu.VMEM((t, d), _F32),
        ],
        compiler_params=pltpu.CompilerParams(vmem_limit_bytes=V7X_VMEM_LIMIT_BYTES),
        name="token_mix",
    )(x, *mix_vmem_in, *mix_hbm_in)

    m = bsz * s_len
    t_ffn = FFN_SUB_TILES * t
    assert m % t_ffn == 0
    ffn_vmem_in = (row(norm2_pre_g), row(norm2_post_g))
    in_vmem = pl.BlockSpec(memory_space=pltpu.VMEM)
    out = pl.pallas_call(
        _ffn_kernel,
        in_specs=[_IN_HBM] + [in_vmem] * len(ffn_vmem_in) + [_IN_HBM] * 2,
        out_specs=_IN_HBM,
        out_shape=jax.ShapeDtypeStruct((m, d), x.dtype),
        scratch_shapes=[
            pltpu.VMEM((d, d_ff), _BF16),
            pltpu.VMEM((d_ff, d), _BF16),
            pltpu.VMEM(stage_shape, _F32),
            pltpu.SemaphoreType.DMA((W_STAGE_SLOTS,)),
            pltpu.VMEM((t_ffn, d), _BF16),
            pltpu.VMEM((t_ffn, d_ff), _BF16),
            pltpu.VMEM((t_ffn, d), _F32),
        ],
        compiler_params=pltpu.CompilerParams(vmem_limit_bytes=V7X_VMEM_LIMIT_BYTES),
        name="channel_mix",
    )(h.reshape(m, d), *ffn_vmem_in, w_ff1, w_ff2)
    return out.reshape(bsz, s_len, d)
```
